```python
import math, functools
import jax, jax.numpy as jnp
from jax import lax
import numpy as np

D_MODEL = 1024
BATCH = 8
SEQ = 2048
DEPTH = 1
DEC_BATCH = 128
DEC_SEQ = 4
PAST_LEN = 8192
PAGE_SIZE = 128

N_HEADS = 8
HEAD_DIM = 64
V_DIM = 2 * HEAD_DIM
QK_WIDTH = N_HEADS * 2 * HEAD_DIM
ATTN_WIDTH = N_HEADS * V_DIM
SSM_GROUP = 16
SSM_GROUPS = D_MODEL // SSM_GROUP
SSM_WIDTH = SSM_GROUP * SSM_GROUPS
SSM_STATE = 64
SSM_CHUNK = 128
D_FF = 2816
CONV_W = 3
Q_BLOCK = 128
LN_EPS = 1e-5
DT_MIN = 1e-3
DT_MAX = 1e-1
DEEPNORM_ALPHA = (2.0 * DEPTH) ** 0.25
DEEPNORM_BETA = (8.0 * DEPTH) ** -0.25
SPLIT_IDX = [QK_WIDTH, 2 * QK_WIDTH, 2 * QK_WIDTH + ATTN_WIDTH,
             2 * QK_WIDTH + ATTN_WIDTH + SSM_WIDTH,
             2 * QK_WIDTH + 2 * ATTN_WIDTH + SSM_WIDTH]

kernel_name = "diffattn_s5_convffn_deepnorm_step"

F32 = jnp.float32


def layer_norm(x, g, b):
    xf = x.astype(F32)
    mu = jnp.mean(xf, -1, keepdims=True)
    var = jnp.mean(jnp.square(xf - mu), -1, keepdims=True)
    return ((xf - mu) * lax.rsqrt(var + LN_EPS) * g + b).astype(x.dtype)


def rms_norm(x, g):
    xf = x.astype(F32)
    return (xf * lax.rsqrt(jnp.mean(xf * xf, -1, keepdims=True) + LN_EPS) * g).astype(x.dtype)


def diff_attn_prompt(q, k, v, lam):
    b, L = q.shape[:2]
    scale = HEAD_DIM ** -0.5
    kpos = jnp.arange(L)

    def block(i):
        qb = lax.dynamic_slice_in_dim(q, i * Q_BLOCK, Q_BLOCK, axis=1)
        s = jnp.einsum('bqhcd,bkhcd->bhcqk', qb, k).astype(F32) * scale
        qpos = i * Q_BLOCK + jnp.arange(Q_BLOCK)
        s = jnp.where(kpos[None, :] <= qpos[:, None], s, -jnp.inf)
        p = jax.nn.softmax(s, axis=-1)
        w = p[:, :, 0] - lam * p[:, :, 1]
        return jnp.einsum('bhqk,bkhe->bqhe', w.astype(v.dtype), v)

    o = lax.map(block, jnp.arange(L // Q_BLOCK))
    return jnp.moveaxis(o, 0, 1).reshape(b, L, N_HEADS, V_DIM)


def diff_attn_sample(q, k, v, lam, *, cache_k, cache_v, page_table, layer):
    bd, T = q.shape[:2]
    scale = HEAD_DIM ** -0.5

    def update(carry, s, vb):
        m, l, acc = carry
        m_new = jnp.maximum(m, s.max(-1))
        corr = jnp.exp(m - m_new)
        p = jnp.exp(s - m_new[..., None])
        l = l * corr + p.sum(-1)
        acc = acc * corr[..., None] + jnp.einsum('bhcqp,bphe->bhcqe', p, vb.astype(F32))
        return (m_new, l, acc)

    def step(carry, n):
        ids = page_table[:, n]
        kp = cache_k[layer, ids].reshape(bd, PAGE_SIZE, N_HEADS, 2, HEAD_DIM)
        vp = cache_v[layer, ids]
        s = jnp.einsum('bqhcd,bphcd->bhcqp', q, kp).astype(F32) * scale
        return update(carry, s, vp), None

    init = (jnp.full((bd, N_HEADS, 2, T), -jnp.inf, F32),
            jnp.zeros((bd, N_HEADS, 2, T), F32),
            jnp.zeros((bd, N_HEADS, 2, T, V_DIM), F32))
    carry, _ = lax.scan(step, init, jnp.arange(page_table.shape[1]))
    s_new = jnp.einsum('bqhcd,bphcd->bhcqp', q, k).astype(F32) * scale
    causal = jnp.tril(jnp.ones((T, T), bool))
    s_new = jnp.where(causal, s_new, -jnp.inf)
    m, l, acc = update(carry, s_new, v)
    out = acc / l[..., None]
    o = out[:, :, 0] - lam * out[:, :, 1]
    return jnp.transpose(o, (0, 2, 1, 3)).astype(v.dtype)


def ssm_discretize(a_re, a_im, log_dt, b_re, b_im):
    dt = jnp.exp(log_dt.astype(F32))[:, None]
    ar, ai = a_re.astype(F32), a_im.astype(F32)
    mag = jnp.exp(ar * dt)
    lb_re, lb_im = mag * jnp.cos(ai * dt), mag * jnp.sin(ai * dt)
    den = ar * ar + ai * ai
    nr, ni = lb_re - 1.0, lb_im
    cr = ((nr * ar + ni * ai) / den)[..., None]
    ci = ((ni * ar - nr * ai) / den)[..., None]
    br, bi = b_re.astype(F32), b_im.astype(F32)
    return lb_re, lb_im, cr * br - ci * bi, cr * bi + ci * br


def ssm_combine(e1, e2):
    a1r, a1i, b1r, b1i = e1
    a2r, a2i, b2r, b2i = e2
    return (a2r * a1r - a2i * a1i, a2r * a1i + a2i * a1r,
            a2r * b1r - a2i * b1i + b2r, a2r * b1i + a2i * b1r + b2i)


def ssm_block(u, h_re, h_im, lb_re, lb_im, bb_re, bb_im, c_re, c_im, d_skip):
    uf = u.astype(F32)
    bu_re = jnp.einsum('btgc,gnc->btgn', uf, bb_re)
    bu_im = jnp.einsum('btgc,gnc->btgn', uf, bb_im)
    bu_re = bu_re.at[:, 0].add(lb_re * h_re - lb_im * h_im)
    bu_im = bu_im.at[:, 0].add(lb_re * h_im + lb_im * h_re)
    a_re = jnp.broadcast_to(lb_re, bu_re.shape)
    a_im = jnp.broadcast_to(lb_im, bu_im.shape)
    _, _, hs_re, hs_im = lax.associative_scan(ssm_combine, (a_re, a_im, bu_re, bu_im), axis=1)
    y = (jnp.einsum('btgn,gcn->btgc', hs_re, c_re.astype(F32))
         - jnp.einsum('btgn,gcn->btgc', hs_im, c_im.astype(F32))
         + d_skip.astype(F32) * uf)
    return y, hs_re[:, -1], hs_im[:, -1]


def run_ssm(u, h_re, h_im, lb_re, lb_im, bb_re, bb_im, c_re, c_im, d_skip):
    b, L = u.shape[:2]
    chunk = SSM_CHUNK if L % SSM_CHUNK == 0 else L
    uc = jnp.moveaxis(u.reshape(b, L // chunk, chunk, SSM_GROUPS, SSM_GROUP), 1, 0)

    def step(carry, ub):
        y, hr, hi = ssm_block(ub, carry[0], carry[1], lb_re, lb_im, bb_re, bb_im, c_re, c_im, d_skip)
        return (hr, hi), y

    (hr, hi), ys = lax.scan(step, (h_re.astype(F32), h_im.astype(F32)), uc)
    return jnp.moveaxis(ys, 0, 1).reshape(b, L, SSM_WIDTH), hr, hi


def conv_ffn(x, buf, w_up, conv_w, conv_b, w_down):
    L = x.shape[1]
    up = x @ w_up
    a, b = jnp.split(up, 2, axis=-1)
    ap = jnp.concatenate([buf.astype(a.dtype), a], axis=1)
    conv = conv_b + ap[:, 0:L] * conv_w[0]
    for j in range(1, CONV_W):
        conv = conv + ap[:, j:j + L] * conv_w[j]
    h = jax.nn.gelu(conv) * b
    return h @ w_down, ap[:, -(CONV_W - 1):]


def layer_forward(x, attend, h_re, h_im, conv_buf, lam_init,
                  w_in, lambda_q1, lambda_k1, lambda_q2, lambda_k2, subln_g,
                  ssm_a_re, ssm_a_im, ssm_log_dt, ssm_b_re, ssm_b_im, ssm_c_re, ssm_c_im, ssm_d,
                  w_glu, b_glu, w_out, ln1_g, ln1_b, w_up, conv_w, conv_b, w_down, ln2_g, ln2_b):
    b, L, _ = x.shape
    proj = x @ w_in
    q, k, v, u, g_att, g_ssm = jnp.split(proj, SPLIT_IDX, axis=-1)
    q = q.reshape(b, L, N_HEADS, 2, HEAD_DIM)
    k = k.reshape(b, L, N_HEADS, 2, HEAD_DIM)
    v = v.reshape(b, L, N_HEADS, V_DIM)
    lam = (jnp.exp(jnp.sum(lambda_q1.astype(F32) * lambda_k1.astype(F32)))
           - jnp.exp(jnp.sum(lambda_q2.astype(F32) * lambda_k2.astype(F32))) + lam_init)
    o = attend(q, k, v, lam)
    o = (rms_norm(o, subln_g) * (1.0 - lam_init)).reshape(b, L, ATTN_WIDTH)
    lb_re, lb_im, bb_re, bb_im = ssm_discretize(ssm_a_re, ssm_a_im, ssm_log_dt, ssm_b_re, ssm_b_im)
    y, hr, hi = run_ssm(u.reshape(b, L, SSM_GROUPS, SSM_GROUP), h_re, h_im,
                        lb_re, lb_im, bb_re, bb_im, ssm_c_re, ssm_c_im, ssm_d)
    gl = jax.nn.gelu(y.astype(x.dtype))
    s = gl * jax.nn.sigmoid(gl @ w_glu + b_glu)
    merged = jax.nn.sigmoid(g_att) * o + jax.nn.sigmoid(g_ssm) * s
    x1 = layer_norm(DEEPNORM_ALPHA * x + merged @ w_out, ln1_g, ln1_b)
    f, new_buf = conv_ffn(x1, conv_buf, w_up, conv_w, conv_b, w_down)
    x2 = layer_norm(DEEPNORM_ALPHA * x1 + f, ln2_g, ln2_b)
    return x2, k.reshape(b, L, N_HEADS, 2 * HEAD_DIM), v, hr, hi, new_buf


def setup_inputs(seed: int = 0) -> dict:
    key = jax.random.key(seed)
    ks = jax.random.split(key, 40)
    n_pages = PAST_LEN // PAGE_SIZE
    n_pool = (DEC_BATCH * n_pages * 5) // 4

    def nrm(k, shape, scale):
        return jax.random.normal(k, shape, F32) * scale

    s_in = D_MODEL ** -0.5
    x_prompt = nrm(ks[0], (BATCH, SEQ, D_MODEL), 1.0)
    x_sample = nrm(ks[1], (DEC_BATCH, DEC_SEQ, D_MODEL), 1.0)
    cache_k = nrm(ks[2], (DEPTH, n_pool, PAGE_SIZE, N_HEADS, 2 * HEAD_DIM), 1.0)
    cache_v = nrm(ks[3], (DEPTH, n_pool, PAGE_SIZE, N_HEADS, V_DIM), DEEPNORM_BETA)
    state_ssm_re = nrm(ks[4], (DEPTH, DEC_BATCH, SSM_GROUPS, SSM_STATE), 0.1)
    state_ssm_im = nrm(ks[5], (DEPTH, DEC_BATCH, SSM_GROUPS, SSM_STATE), 0.1)
    state_conv = nrm(ks[6], (DEPTH, DEC_BATCH, CONV_W - 1, D_FF), DEEPNORM_BETA)
    page_table = jax.random.permutation(ks[7], n_pool)[:DEC_BATCH * n_pages].reshape(
        DEC_BATCH, n_pages).astype(jnp.int32)
    w_in = jnp.concatenate([
        nrm(ks[8], (DEPTH, D_MODEL, QK_WIDTH), s_in),
        nrm(ks[9], (DEPTH, D_MODEL, QK_WIDTH), s_in),
        nrm(ks[10], (DEPTH, D_MODEL, ATTN_WIDTH), s_in * DEEPNORM_BETA),
        nrm(ks[11], (DEPTH, D_MODEL, SSM_WIDTH), s_in),
        nrm(ks[12], (DEPTH, D_MODEL, ATTN_WIDTH), s_in),
        nrm(ks[13], (DEPTH, D_MODEL, SSM_WIDTH), s_in)], axis=-1)
    lambda_q1 = nrm(ks[14], (DEPTH, HEAD_DIM), 0.1)
    lambda_k1 = nrm(ks[15], (DEPTH, HEAD_DIM), 0.1)
    lambda_q2 = nrm(ks[16], (DEPTH, HEAD_DIM), 0.1)
    lambda_k2 = nrm(ks[17], (DEPTH, HEAD_DIM), 0.1)
    subln_g = 1.0 + nrm(ks[18], (DEPTH, V_DIM), 0.02)
    ssm_a_re = -0.5 + nrm(ks[19], (DEPTH, SSM_GROUPS, SSM_STATE), 0.01)
    ssm_a_im = (math.pi * jnp.arange(SSM_STATE, dtype=F32)
                + nrm(ks[20], (DEPTH, SSM_GROUPS, SSM_STATE), 0.01))
    ssm_log_dt = jax.random.uniform(ks[21], (DEPTH, SSM_GROUPS), F32,
                                    math.log(DT_MIN), math.log(DT_MAX))
    ssm_b_re = nrm(ks[22], (DEPTH, SSM_GROUPS, SSM_STATE, SSM_GROUP), SSM_GROUP ** -0.5)
    ssm_b_im = nrm(ks[23], (DEPTH, SSM_GROUPS, SSM_STATE, SSM_GROUP), SSM_GROUP ** -0.5)
    ssm_c_re = nrm(ks[24], (DEPTH, SSM_GROUPS, SSM_GROUP, SSM_STATE), SSM_STATE ** -0.5)
    ssm_c_im = nrm(ks[25], (DEPTH, SSM_GROUPS, SSM_GROUP, SSM_STATE), SSM_STATE ** -0.5)
    ssm_d = nrm(ks[26], (DEPTH, SSM_GROUPS, SSM_GROUP), 0.5)
    w_glu = nrm(ks[27], (DEPTH, SSM_WIDTH, SSM_WIDTH), SSM_WIDTH ** -0.5)
    b_glu = nrm(ks[28], (DEPTH, SSM_WIDTH), 0.01)
    w_out = nrm(ks[29], (DEPTH, ATTN_WIDTH, D_MODEL), ATTN_WIDTH ** -0.5 * DEEPNORM_BETA)
    ln1_g = 1.0 + nrm(ks[30], (DEPTH, D_MODEL), 0.02)
    ln1_b = nrm(ks[31], (DEPTH, D_MODEL), 0.01)
    w_up = nrm(ks[32], (DEPTH, D_MODEL, 2 * D_FF), s_in * DEEPNORM_BETA)
    conv_w = nrm(ks[33], (DEPTH, CONV_W, D_FF), CONV_W ** -0.5)
    conv_b = nrm(ks[34], (DEPTH, D_FF), 0.01)
    w_down = nrm(ks[35], (DEPTH, D_FF, D_MODEL), D_FF ** -0.5 * DEEPNORM_BETA)
    ln2_g = 1.0 + nrm(ks[36], (DEPTH, D_MODEL), 0.02)
    ln2_b = nrm(ks[37], (DEPTH, D_MODEL), 0.01)
    return {"x_prompt": x_prompt, "x_sample": x_sample, "cache_k": cache_k, "cache_v": cache_v,
            "state_ssm_re": state_ssm_re, "state_ssm_im": state_ssm_im, "state_conv": state_conv,
            "page_table": page_table, "w_in": w_in, "lambda_q1": lambda_q1, "lambda_k1": lambda_k1,
            "lambda_q2": lambda_q2, "lambda_k2": lambda_k2, "subln_g": subln_g,
            "ssm_a_re": ssm_a_re, "ssm_a_im": ssm_a_im, "ssm_log_dt": ssm_log_dt,
            "ssm_b_re": ssm_b_re, "ssm_b_im": ssm_b_im, "ssm_c_re": ssm_c_re, "ssm_c_im": ssm_c_im,
            "ssm_d": ssm_d, "w_glu": w_glu, "b_glu": b_glu, "w_out": w_out,
            "ln1_g": ln1_g, "ln1_b": ln1_b, "w_up": w_up, "conv_w": conv_w, "conv_b": conv_b,
            "w_down": w_down, "ln2_g": ln2_g, "ln2_b": ln2_b}


def reference(x_prompt, x_sample, cache_k, cache_v, state_ssm_re, state_ssm_im, state_conv,
              page_table, w_in, lambda_q1, lambda_k1, lambda_q2, lambda_k2, subln_g,
              ssm_a_re, ssm_a_im, ssm_log_dt, ssm_b_re, ssm_b_im, ssm_c_re, ssm_c_im, ssm_d,
              w_glu, b_glu, w_out, ln1_g, ln1_b, w_up, conv_w, conv_b, w_down, ln2_g, ln2_b):
    yp, ys = x_prompt, x_sample
    bp = x_prompt.shape[0]
    kp_l, vp_l, ks_l, vs_l = [], [], [], []
    hrp_l, hip_l, hrs_l, his_l, cp_l, cs_l = [], [], [], [], [], []
    for layer in range(DEPTH):
        lam_init = 0.8 - 0.6 * math.exp(-0.3 * layer)
        weights = (w_in[layer], lambda_q1[layer], lambda_k1[layer], lambda_q2[layer], lambda_k2[layer],
                   subln_g[layer], ssm_a_re[layer], ssm_a_im[layer], ssm_log_dt[layer],
                   ssm_b_re[layer], ssm_b_im[layer], ssm_c_re[layer], ssm_c_im[layer], ssm_d[layer],
                   w_glu[layer], b_glu[layer], w_out[layer], ln1_g[layer], ln1_b[layer],
                   w_up[layer], conv_w[layer], conv_b[layer], w_down[layer], ln2_g[layer], ln2_b[layer])
        h0 = jnp.zeros((bp, SSM_GROUPS, SSM_STATE), F32)
        buf0 = jnp.zeros((bp, CONV_W - 1, D_FF), yp.dtype)
        yp, kp, vp, hrp, hip, cbp = layer_forward(yp, diff_attn_prompt, h0, h0, buf0, lam_init, *weights)
        attend_s = functools.partial(diff_attn_sample, cache_k=cache_k, cache_v=cache_v,
                                     page_table=page_table, layer=layer)
        ys, ks, vs, hrs, his, cbs = layer_forward(ys, attend_s, state_ssm_re[layer], state_ssm_im[layer],
                                                  state_conv[layer], lam_init, *weights)
        kp_l.append(kp); vp_l.append(vp); ks_l.append(ks); vs_l.append(vs)
        hrp_l.append(hrp); hip_l.append(hip); hrs_l.append(hrs); his_l.append(his)
        cp_l.append(cbp); cs_l.append(cbs)
    k_prompt, v_prompt = jnp.stack(kp_l), jnp.stack(vp_l)
    k_sample, v_sample = jnp.stack(ks_l), jnp.stack(vs_l)
    ssm_re_prompt, ssm_im_prompt = jnp.stack(hrp_l), jnp.stack(hip_l)
    ssm_re_sample, ssm_im_sample = jnp.stack(hrs_l), jnp.stack(his_l)
    conv_prompt, conv_sample = jnp.stack(cp_l), jnp.stack(cs_l)
    return (yp, ys, k_prompt, v_prompt, k_sample, v_sample,
            ssm_re_prompt, ssm_im_prompt, ssm_re_sample, ssm_im_sample, conv_prompt, conv_sample)
```

```python
import functools
import math

import jax
import jax.numpy as jnp
from jax import lax
from jax.experimental import pallas as pl
from jax.experimental.pallas import tpu as pltpu

F32 = jnp.float32
BF16 = jnp.bfloat16

N_HEADS = 8
HEAD_DIM = 64
V_DIM = 2 * HEAD_DIM
D_MODEL = N_HEADS * V_DIM
SSM_GROUP = 16
SSM_GROUPS = D_MODEL // SSM_GROUP
SSM_STATE = 64
CONV_W = 3
PAGE_SIZE = 128
LN_EPS = 1e-5
QK_SCALE = HEAD_DIM ** -0.5

LANES = 128
MXU_DIM = 256
VMEM_LIMIT = 56 * 1024 * 1024

GROUPS_PER_BLOCK = MXU_DIM // SSM_GROUP
STATE_BLOCK = GROUPS_PER_BLOCK * SSM_STATE
N_GROUP_BLOCKS = SSM_GROUPS // GROUPS_PER_BLOCK
FF_CHUNK = MXU_DIM


def _params(*sem):
    return pltpu.CompilerParams(dimension_semantics=sem, vmem_limit_bytes=VMEM_LIMIT)


def _full(shape):
    return pl.BlockSpec(shape, lambda *_: (0,) * len(shape))


def _gelu(x):
    return x * (0.5 * (1.0 + jnp.tanh(math.sqrt(2.0 / math.pi) * (x + 0.044715 * (x * x * x)))))


def _sigmoid(x):
    return 1.0 / (1.0 + jnp.exp(-x))


def _layer_norm(x, g, b):
    mu = jnp.mean(x, -1, keepdims=True)
    xc = x - mu
    var = jnp.mean(xc * xc, -1, keepdims=True)
    return xc * lax.rsqrt(var + LN_EPS) * g + b


def _lambda(lq1, lk1, lq2, lk2, lam_init):
    return (jnp.exp(jnp.sum(lq1[...] * lk1[...], keepdims=True))
            - jnp.exp(jnp.sum(lq2[...] * lk2[...], keepdims=True)) + lam_init)


def _inproj_kernel(x_ref, w_ref, q_ref, kf_ref, kb_ref, vf_ref, vb_ref, u_ref, ga_ref, gs_ref):
    x = x_ref[...].astype(BF16)

    def section(j):
        return jnp.dot(x, w_ref[:, j * D_MODEL:(j + 1) * D_MODEL], preferred_element_type=F32)

    q_ref[...] = (section(0) * QK_SCALE).astype(BF16)
    k = section(1)
    kf_ref[...] = k
    kb_ref[...] = k.astype(BF16)
    v = section(2)
    vf_ref[...] = v
    vb_ref[...] = v.astype(BF16)
    u_ref[...] = section(3)
    ga_ref[...] = section(4)
    gs_ref[...] = section(5)


def _inproj(x, w_in, tm):
    nb, seq, d = x.shape
    tmaj = pl.BlockSpec((tm, d), lambda b, i: (i, b))
    bmaj = pl.BlockSpec((None, tm, d), lambda b, i: (b, i, 0))
    t_shape = lambda dt: jax.ShapeDtypeStruct((seq, nb * d), dt)
    b_shape = jax.ShapeDtypeStruct((nb, seq, d), F32)
    return pl.pallas_call(
        _inproj_kernel,
        grid=(nb, seq // tm),
        in_specs=[bmaj, _full(w_in.shape)],
        out_specs=[tmaj, bmaj, tmaj, bmaj, tmaj, tmaj, tmaj, tmaj],
        out_shape=[t_shape(BF16), b_shape, t_shape(BF16), b_shape, t_shape(BF16),
                   t_shape(F32), t_shape(F32), t_shape(F32)],
        compiler_params=_params("parallel", "parallel"),
        name="inproj",
    )(x, w_in)


def _attn_prompt_kernel(lq1, lk1, lq2, lk2, g_ref, q_ref, k_ref, v_ref, o_ref,
                        m_sc, l_sc, acc_sc, *, tq, lam_init):
    i = pl.program_id(2)
    lam = _lambda(lq1, lk1, lq2, lk2, lam_init)
    q = q_ref[...]
    lane = lax.broadcasted_iota(jnp.int32, q.shape, 1)
    zero = jnp.zeros_like(q)
    qq = jnp.concatenate([jnp.where(lane < HEAD_DIM, q, zero),
                          jnp.where(lane >= HEAD_DIM, q, zero)], axis=0)
    m_sc[...] = jnp.full(m_sc.shape, -jnp.inf, F32)
    l_sc[...] = jnp.zeros(l_sc.shape, F32)
    acc_sc[...] = jnp.zeros(acc_sc.shape, F32)

    def update(j, masked):
        start = pl.multiple_of(j * tq, tq)
        kk = k_ref[pl.ds(start, tq), :]
        vv = v_ref[pl.ds(start, tq), :]
        s = lax.dot_general(qq, kk, (((1,), (1,)), ((), ())), preferred_element_type=F32)
        if masked:
            row = lax.broadcasted_iota(jnp.int32, s.shape, 0)
            col = lax.broadcasted_iota(jnp.int32, s.shape, 1)
            qpos = jnp.where(row >= tq, row - tq, row)
            s = jnp.where(col <= qpos, s, -jnp.inf)
        m_prev = m_sc[...]
        m_new = jnp.maximum(m_prev, jnp.max(s, -1, keepdims=True))
        corr = jnp.exp(m_prev - m_new)
        p = jnp.exp(s - m_new)
        l_sc[...] = l_sc[...] * corr + jnp.sum(p, -1, keepdims=True)
        acc_sc[...] = acc_sc[...] * corr + jnp.dot(p.astype(BF16), vv, preferred_element_type=F32)
        m_sc[...] = m_new

    def body(j, carry):
        update(j, False)
        return carry

    lax.fori_loop(0, i, body, 0)
    update(i, True)
    out = acc_sc[...] / l_sc[...]
    o = out[:tq] - lam * out[tq:]
    ms = jnp.mean(o * o, -1, keepdims=True)
    o_ref[...] = o * lax.rsqrt(ms + LN_EPS) * g_ref[...] * (1.0 - lam_init)


def _attn_prompt(q, k, v, lams, subln_g, nb, lam_init, tq):
    seq = q.shape[0]
    qspec = pl.BlockSpec((tq, V_DIM), lambda b, h, i: (i, b * N_HEADS + h))
    kvspec = pl.BlockSpec((seq, V_DIM), lambda b, h, i: (0, b * N_HEADS + h))
    vec = _full((1, HEAD_DIM))
    return pl.pallas_call(
        functools.partial(_attn_prompt_kernel, tq=tq, lam_init=lam_init),
        grid=(nb, N_HEADS, seq // tq),
        in_specs=[vec, vec, vec, vec, _full((1, V_DIM)), qspec, kvspec, kvspec],
        out_specs=qspec,
        out_shape=jax.ShapeDtypeStruct(q.shape, F32),
        scratch_shapes=[pltpu.VMEM((2 * tq, 1), F32), pltpu.VMEM((2 * tq, 1), F32),
                        pltpu.VMEM((2 * tq, V_DIM), F32)],
        compiler_params=_params("parallel", "parallel", "arbitrary"),
        name="attn_prompt",
    )(*lams, subln_g, q, k, v)


ROWS = N_HEADS * 2 * 4


def _attn_sample_kernel(pt_ref, q8_ref, *refs, pages_per_step, lam_init):
    del pt_ref
    k_refs = refs[:pages_per_step]
    v_refs = refs[pages_per_step:2 * pages_per_step]
    (kn_ref, vn_ref, lq1, lk1, lq2, lk2, g_ref, o_ref,
     qbd_sc, m_sc, l_sc, acc_sc) = refs[2 * pages_per_step:]
    n = pl.program_id(1)

    @pl.when(n == 0)
    def _():
        qt = jnp.concatenate([q8_ref[...]] * N_HEADS, axis=0)
        row = lax.broadcasted_iota(jnp.int32, qt.shape, 0)
        col = lax.broadcasted_iota(jnp.int32, qt.shape, 1)
        qbd_sc[...] = jnp.where((col >> 6) == (row >> 2), qt, 0.0).astype(BF16)
        m_sc[...] = jnp.full(m_sc.shape, -jnp.inf, F32)
        l_sc[...] = jnp.zeros(l_sc.shape, F32)
        acc_sc[...] = jnp.zeros(acc_sc.shape, F32)

    qbd = qbd_sc[...]

    def scores(kk):
        return lax.dot_general(qbd, kk, (((1,), (1,)), ((), ())), preferred_element_type=F32)

    def head_diag(pv):
        return jnp.concatenate(
            [pv[h * 8:(h + 1) * 8, h * V_DIM:(h + 1) * V_DIM] for h in range(N_HEADS)], axis=0)

    def update(s_list, v_list):
        m_prev = m_sc[...]
        m_new = m_prev
        for s in s_list:
            m_new = jnp.maximum(m_new, jnp.max(s, -1, keepdims=True))
        corr = jnp.exp(m_prev - m_new)
        l_new = l_sc[...] * corr
        pv = None
        for s, vv in zip(s_list, v_list):
            p = jnp.exp(s - m_new)
            l_new = l_new + jnp.sum(p, -1, keepdims=True)
            d = jnp.dot(p.astype(BF16), vv, preferred_element_type=F32)
            pv = d if pv is None else pv + d
        l_sc[...] = l_new
        acc_sc[...] = acc_sc[...] * corr + head_diag(pv)
        m_sc[...] = m_new

    update([scores(k_refs[p][...].astype(BF16)) for p in range(pages_per_step)],
           [v_refs[p][...].astype(BF16) for p in range(pages_per_step)])

    @pl.when(n == pl.num_programs(1) - 1)
    def _():
        s = scores(kn_ref[...])
        row = lax.broadcasted_iota(jnp.int32, s.shape, 0)
        col = lax.broadcasted_iota(jnp.int32, s.shape, 1)
        s = jnp.where(col <= (row & 3), s, -jnp.inf)
        update([s], [vn_ref[...]])
        lam = _lambda(lq1, lk1, lq2, lk2, lam_init)
        out = acc_sc[...] / l_sc[...]
        for h in range(N_HEADS):
            oh = out[h * 8:(h + 1) * 8]
            o = oh - lam * pltpu.roll(oh, 4, axis=0)
            ms = jnp.mean(o * o, -1, keepdims=True)
            o_ref[:, h * V_DIM:(h + 1) * V_DIM] = (
                o * lax.rsqrt(ms + LN_EPS) * g_ref[...] * (1.0 - lam_init))


def _attn_sample(q8, k_new, v_new, cache_k, cache_v, page_table, lams, subln_g, lam_init, pages_per_step):
    nbd, n_pages = page_table.shape
    width = cache_k.shape[-1]

    def page_spec(p):
        return pl.BlockSpec((None, PAGE_SIZE, width),
                            lambda b, n, pt: (pt[b, n * pages_per_step + p], 0, 0))

    per_b = lambda rows: pl.BlockSpec((None, rows, width), lambda b, n, pt: (b, 0, 0))
    vec = pl.BlockSpec((1, HEAD_DIM), lambda b, n, pt: (0, 0))
    grid_spec = pltpu.PrefetchScalarGridSpec(
        num_scalar_prefetch=1,
        grid=(nbd, n_pages // pages_per_step),
        in_specs=([per_b(8)] + [page_spec(p) for p in range(pages_per_step)] * 2
                  + [per_b(k_new.shape[1]), per_b(v_new.shape[1]), vec, vec, vec, vec,
                     pl.BlockSpec((1, V_DIM), lambda b, n, pt: (0, 0))]),
        out_specs=per_b(8),
        scratch_shapes=[pltpu.VMEM((ROWS, width), BF16), pltpu.VMEM((ROWS, 1), F32),
                        pltpu.VMEM((ROWS, 1), F32), pltpu.VMEM((ROWS, V_DIM), F32)],
    )
    return pl.pallas_call(
        functools.partial(_attn_sample_kernel, pages_per_step=pages_per_step, lam_init=lam_init),
        grid_spec=grid_spec,
        out_shape=jax.ShapeDtypeStruct((nbd, 8, width), F32),
        compiler_params=_params("parallel", "arbitrary"),
        name="attn_sample",
    )(page_table, q8, *([cache_k] * pages_per_step), *([cache_v] * pages_per_step),
      k_new, v_new, *lams, subln_g)


def _discretize_kernel(ar_ref, ai_ref, ldt_ref, br_ref, bi_ref, lbr_ref, lbi_ref, bbr_ref, bbi_ref):
    dt = jnp.exp(ldt_ref[...])
    ar, ai = ar_ref[...], ai_ref[...]
    mag = jnp.exp(ar * dt)
    lb_re, lb_im = mag * jnp.cos(ai * dt), mag * jnp.sin(ai * dt)
    den = ar * ar + ai * ai
    nr, ni = lb_re - 1.0, lb_im
    cr = (nr * ar + ni * ai) / den
    ci = (ni * ar - nr * ai) / den
    br, bi = br_ref[...], bi_ref[...]
    lbr_ref[...] = lb_re
    lbi_ref[...] = lb_im
    bbr_ref[...] = cr * br - ci * bi
    bbi_ref[...] = cr * bi + ci * br


def _discretize(a_re, a_im, log_dt, b_re, b_im):
    rep = lambda a: jnp.repeat(a, SSM_GROUP, axis=0)
    to_rows = lambda b: jnp.transpose(b, (0, 2, 1)).reshape(SSM_GROUPS * SSM_GROUP, SSM_STATE)
    shape = jax.ShapeDtypeStruct((SSM_GROUPS * SSM_GROUP, SSM_STATE), F32)
    lbr, lbi, bbr, bbi = pl.pallas_call(
        _discretize_kernel, out_shape=[shape] * 4, name="ssm_discretize",
    )(rep(a_re), rep(a_im), rep(log_dt[:, None]), to_rows(b_re), to_rows(b_im))
    unrows = lambda b: b.reshape(SSM_GROUPS, SSM_GROUP, SSM_STATE)
    return lbr[::SSM_GROUP], lbi[::SSM_GROUP], unrows(bbr), unrows(bbi)


def _block_diag_weights(bb_re, bb_im, c_re, c_im):
    eye = jnp.eye(GROUPS_PER_BLOCK, dtype=F32)

    def in_w(bb):
        bb = bb.reshape(N_GROUP_BLOCKS, GROUPS_PER_BLOCK, SSM_GROUP, SSM_STATE)
        w = bb[:, :, :, None, :] * eye[None, :, None, :, None]
        return w.reshape(N_GROUP_BLOCKS, MXU_DIM, STATE_BLOCK)

    def out_w(c):
        c = c.reshape(N_GROUP_BLOCKS, GROUPS_PER_BLOCK, SSM_GROUP, SSM_STATE)
        w = jnp.transpose(c, (0, 1, 3, 2))[:, :, :, None, :] * eye[None, :, None, :, None]
        return w.reshape(N_GROUP_BLOCKS, STATE_BLOCK, MXU_DIM)

    w_b = jnp.concatenate([in_w(bb_re), in_w(bb_im)], axis=2).astype(BF16)
    w_c = jnp.concatenate([out_w(c_re), out_w(-c_im)], axis=1).astype(BF16)
    return w_b, w_c


def _ssm_kernel(u_ref, h0r_ref, h0i_ref, lbr_ref, lbi_ref, wb_ref, wc_ref, d_ref,
                y_ref, hr_ref, hi_ref, hs_sc, sr_sc, si_sc, *, nb, steps, unroll):
    j = pl.program_id(1)

    @pl.when(j == 0)
    def _():
        sr_sc[...] = h0r_ref[...]
        si_sc[...] = h0i_ref[...]

    u = u_ref[...]
    hs_sc[...] = jnp.dot(u.astype(BF16), wb_ref[...], preferred_element_type=F32)
    lbr, lbi = lbr_ref[...], lbi_ref[...]

    def step(t, carry):
        hr, hi = carry
        r0 = pl.multiple_of(t * nb, nb)
        bur = hs_sc[pl.ds(r0, nb), :STATE_BLOCK]
        bui = hs_sc[pl.ds(r0, nb), STATE_BLOCK:]
        nr = lbr * hr - lbi * hi + bur
        ni = lbr * hi + lbi * hr + bui
        hs_sc[pl.ds(r0, nb), :STATE_BLOCK] = nr
        hs_sc[pl.ds(r0, nb), STATE_BLOCK:] = ni
        return nr, ni

    hr, hi = lax.fori_loop(0, steps, step, (sr_sc[...], si_sc[...]), unroll=unroll)
    sr_sc[...] = hr
    si_sc[...] = hi
    y_ref[...] = (jnp.dot(hs_sc[...].astype(BF16), wc_ref[...], preferred_element_type=F32)
                  + d_ref[...] * u)

    @pl.when(j == pl.num_programs(1) - 1)
    def _():
        hr_ref[...] = hr
        hi_ref[...] = hi


def _ssm(u, h0_re, h0_im, lb_re, lb_im, w_b, w_c, d_skip, nb, steps):
    rows = u.shape[0]
    tile = steps * nb
    ublk = pl.BlockSpec((tile, MXU_DIM), lambda g, j: (j, g))
    sblk = pl.BlockSpec((nb, STATE_BLOCK), lambda g, j: (0, g))
    lblk = pl.BlockSpec((1, STATE_BLOCK), lambda g, j: (0, g))
    state = jax.ShapeDtypeStruct(h0_re.shape, F32)
    return pl.pallas_call(
        functools.partial(_ssm_kernel, nb=nb, steps=steps, unroll=(4 if nb <= 8 else True)),
        grid=(N_GROUP_BLOCKS, rows // tile),
        in_specs=[ublk, sblk, sblk, lblk, lblk,
                  pl.BlockSpec((None, MXU_DIM, 2 * STATE_BLOCK), lambda g, j: (g, 0, 0)),
                  pl.BlockSpec((None, 2 * STATE_BLOCK, MXU_DIM), lambda g, j: (g, 0, 0)),
                  pl.BlockSpec((1, MXU_DIM), lambda g, j: (0, g))],
        out_specs=[ublk, sblk, sblk],
        out_shape=[jax.ShapeDtypeStruct(u.shape, F32), state, state],
        scratch_shapes=[pltpu.VMEM((tile, 2 * STATE_BLOCK), F32),
                        pltpu.VMEM((nb, STATE_BLOCK), F32), pltpu.VMEM((nb, STATE_BLOCK), F32)],
        compiler_params=_params("parallel", "arbitrary"),
        name="ssm_scan",
    )(u, h0_re, h0_im, lb_re, lb_im, w_b, w_c, d_skip)


def _merge_kernel(x_ref, o_ref, ga_ref, gs_ref, y_ref, wg_ref, bg_ref, wo_ref, g1_ref, b1_ref,
                  x1_ref, *, alpha):
    gl = _gelu(y_ref[...])
    z = jnp.dot(gl.astype(BF16), wg_ref[...], preferred_element_type=F32) + bg_ref[...]
    s = gl * _sigmoid(z)
    merged = _sigmoid(ga_ref[...]) * o_ref[...] + _sigmoid(gs_ref[...]) * s
    r = alpha * x_ref[...] + jnp.dot(merged.astype(BF16), wo_ref[...], preferred_element_type=F32)
    x1_ref[...] = _layer_norm(r, g1_ref[...], b1_ref[...])


def _merge(x, o, ga, gs, y, w_glu, b_glu, w_out, ln_g, ln_b, alpha, tm):
    nb, seq, d = x.shape
    tmaj = pl.BlockSpec((tm, d), lambda b, i: (i, b))
    bmaj = pl.BlockSpec((None, tm, d), lambda b, i: (b, i, 0))
    return pl.pallas_call(
        functools.partial(_merge_kernel, alpha=alpha),
        grid=(nb, seq // tm),
        in_specs=[bmaj, tmaj, tmaj, tmaj, tmaj, _full((d, d)), _full((1, d)), _full((d, d)),
                  _full((1, d)), _full((1, d))],
        out_specs=tmaj,
        out_shape=jax.ShapeDtypeStruct((seq, nb * d), F32),
        compiler_params=_params("parallel", "parallel"),
        name="merge_ln1",
    )(x, o, ga, gs, y, w_glu, b_glu, w_out, ln_g, ln_b)


def _ffn_kernel(x1_ref, buf_ref, wup_ref, cw_ref, cb_ref, wdn_ref, g2_ref, b2_ref,
                x2_ref, nbuf_ref, carry_sc, *, nb, alpha, d_ff):
    i = pl.program_id(0)
    rows = x1_ref.shape[0]

    @pl.when(i == 0)
    def _():
        carry_sc[...] = buf_ref[...]

    x1 = x1_ref[...]
    xb = x1.astype(BF16)
    f = None
    for c in range(d_ff // FF_CHUNK):
        lo, hi = c * FF_CHUNK, (c + 1) * FF_CHUNK
        a = jnp.dot(xb, wup_ref[:, lo:hi], preferred_element_type=F32)
        gate = jnp.dot(xb, wup_ref[:, d_ff + lo:d_ff + hi], preferred_element_type=F32)
        ap = jnp.concatenate([carry_sc[:, lo:hi], a], axis=0)
        conv = (cb_ref[:, lo:hi] + ap[:rows] * cw_ref[0:1, lo:hi]
                + ap[nb:rows + nb] * cw_ref[1:2, lo:hi] + ap[2 * nb:] * cw_ref[2:3, lo:hi])
        carry_sc[:, lo:hi] = ap[rows:]
        h = _gelu(conv) * gate
        d = jnp.dot(h.astype(BF16), wdn_ref[lo:hi, :], preferred_element_type=F32)
        f = d if f is None else f + d
    x2_ref[...] = _layer_norm(alpha * x1 + f, g2_ref[...], b2_ref[...])

    @pl.when(i == pl.num_programs(0) - 1)
    def _():
        nbuf_ref[...] = carry_sc[...]


def _ffn(x1, buf, w_up, conv_w, conv_b, w_down, ln_g, ln_b, nb, alpha, tile):
    rows, d = x1.shape
    d_ff = w_down.shape[0]
    xblk = pl.BlockSpec((tile, d), lambda i: (i, 0))
    return pl.pallas_call(
        functools.partial(_ffn_kernel, nb=nb, alpha=alpha, d_ff=d_ff),
        grid=(rows // tile,),
        in_specs=[xblk, _full(buf.shape), _full(w_up.shape), _full(conv_w.shape), _full((1, d_ff)),
                  _full(w_down.shape), _full((1, d)), _full((1, d))],
        out_specs=[xblk, _full(buf.shape)],
        out_shape=[jax.ShapeDtypeStruct(x1.shape, F32), jax.ShapeDtypeStruct(buf.shape, F32)],
        scratch_shapes=[pltpu.VMEM(buf.shape, F32)],
        compiler_params=_params("arbitrary"),
        name="convffn_ln2",
    )(x1, buf, w_up, conv_w, conv_b, w_down, ln_g, ln_b)


def _layer_forward(x, nb, seq, attend, h_re, h_im, conv_buf, alpha, w, *, row_tile, ssm_steps):
    d = x.shape[-1]
    d_ff = w["w_down"].shape[0]
    q, k_f32, k, v_f32, v, u, ga, gs = _inproj(x, w["w_in"], row_tile // 2)

    o = attend(q, k, v)

    y, hr, hi = _ssm(u.reshape(seq * nb, d), h_re.reshape(nb, -1), h_im.reshape(nb, -1),
                     w["lb_re"], w["lb_im"], w["ssm_w_b"], w["ssm_w_c"], w["ssm_d"], nb, ssm_steps)

    x1 = _merge(x, o, ga, gs, y.reshape(q.shape), w["w_glu"], w["b_glu"], w["w_out"],
                w["ln1_g"], w["ln1_b"], alpha, row_tile)

    buf = jnp.transpose(conv_buf, (1, 0, 2)).reshape((CONV_W - 1) * nb, d_ff)
    x2, nbuf = _ffn(x1.reshape(seq * nb, d), buf, w["w_up"], w["conv_w"], w["conv_b"], w["w_down"],
                    w["ln2_g"], w["ln2_b"], nb, alpha, row_tile)

    nbuf = jnp.transpose(nbuf.reshape(CONV_W - 1, nb, d_ff), (1, 0, 2))
    shape = (nb, SSM_GROUPS, SSM_STATE)
    return x2, k_f32, v_f32, hr.reshape(shape), hi.reshape(shape), nbuf


def kernel(x_prompt, x_sample, cache_k, cache_v, state_ssm_re, state_ssm_im, state_conv, page_table, w_in, lambda_q1, lambda_k1, lambda_q2, lambda_k2, subln_g, ssm_a_re, ssm_a_im, ssm_log_dt, ssm_b_re, ssm_b_im, ssm_c_re, ssm_c_im, ssm_d, w_glu, b_glu, w_out, ln1_g, ln1_b, w_up, conv_w, conv_b, w_down, ln2_g, ln2_b):
    depth = w_in.shape[0]
    alpha = (2.0 * depth) ** 0.25
    bp, seq, d = x_prompt.shape
    bd, dec_seq, _ = x_sample.shape
    d_ff = w_down.shape[1]
    row_tile = 512
    assert dec_seq == 4 and seq % row_tile == 0 and (bd * dec_seq) % row_tile == 0

    yp = x_prompt
    ys = jnp.transpose(x_sample, (1, 0, 2)).reshape(1, dec_seq * bd, d)
    outs = {name: [] for name in ("kp", "vp", "ks", "vs", "hrp", "hip", "hrs", "his", "cp", "cs")}
    for layer in range(depth):
        lam_init = 0.8 - 0.6 * math.exp(-0.3 * layer)
        row = lambda a: a[layer].reshape(1, -1)
        lb_re, lb_im, bb_re, bb_im = _discretize(ssm_a_re[layer], ssm_a_im[layer], ssm_log_dt[layer],
                                                  ssm_b_re[layer], ssm_b_im[layer])
        ssm_w_b, ssm_w_c = _block_diag_weights(bb_re, bb_im, ssm_c_re[layer], ssm_c_im[layer])
        w = dict(
            w_in=w_in[layer].astype(BF16), lb_re=lb_re.reshape(1, -1), lb_im=lb_im.reshape(1, -1),
            ssm_w_b=ssm_w_b, ssm_w_c=ssm_w_c, ssm_d=row(ssm_d),
            w_glu=w_glu[layer].astype(BF16), b_glu=row(b_glu), w_out=w_out[layer].astype(BF16),
            ln1_g=row(ln1_g), ln1_b=row(ln1_b), w_up=w_up[layer].astype(BF16),
            conv_w=conv_w[layer], conv_b=row(conv_b), w_down=w_down[layer].astype(BF16),
            ln2_g=row(ln2_g), ln2_b=row(ln2_b))
        lams = (row(lambda_q1), row(lambda_k1), row(lambda_q2), row(lambda_k2))
        g_sub = row(subln_g)

        attend_p = lambda q, k, v: _attn_prompt(q, k, v, lams, g_sub, bp, lam_init, 256)
        zeros_h = jnp.zeros((bp, SSM_GROUPS, SSM_STATE), F32)
        yp, kp, vp, hrp, hip, cbp = _layer_forward(
            yp, bp, seq, attend_p, zeros_h, zeros_h, jnp.zeros((bp, CONV_W - 1, d_ff), F32), alpha, w,
            row_tile=row_tile, ssm_steps=128)
        yp = jnp.transpose(yp.reshape(seq, bp, d), (1, 0, 2))

        ck = cache_k[layer].reshape(cache_k.shape[1], PAGE_SIZE, d)
        cv = cache_v[layer].reshape(cache_v.shape[1], PAGE_SIZE, d)
        per_seq = lambda a: jnp.transpose(a.reshape(dec_seq, bd, d), (1, 0, 2))

        def attend_s(q, k, v):
            qs = per_seq(q).astype(F32)
            q8 = jnp.concatenate([qs, qs], axis=1)
            pad = lambda a: jnp.pad(per_seq(a), ((0, 0), (0, 16 - dec_seq), (0, 0)))
            o8 = _attn_sample(q8, pad(k), pad(v), ck, cv, page_table, lams, g_sub, lam_init, 8)
            return jnp.transpose(o8[:, :dec_seq], (1, 0, 2)).reshape(dec_seq * bd, d)

        ys, ks, vs, hrs, his, cbs = _layer_forward(
            ys, bd, dec_seq, attend_s, state_ssm_re[layer], state_ssm_im[layer], state_conv[layer],
            alpha, w, row_tile=row_tile, ssm_steps=dec_seq)
        ys = ys.reshape(1, dec_seq * bd, d)
        ks, vs = per_seq(ks), per_seq(vs)
        for name, val in zip(("kp", "vp", "ks", "vs", "hrp", "hip", "hrs", "his", "cp", "cs"),
                             (kp, vp, ks, vs, hrp, hip, hrs, his, cbp, cbs)):
            outs[name].append(val)

    y_sample = per_seq(ys)
    st = lambda name: jnp.stack(outs[name])
    heads = lambda a: a.reshape(a.shape[:-1] + (N_HEADS, V_DIM))
    return (yp, y_sample, heads(st("kp")), heads(st("vp")), heads(st("ks")), heads(st("vs")),
            st("hrp"), st("hip"), st("hrs"), st("his"), st("cp"), st("cs"))
```

```python
import functools
import math

import jax
import jax.numpy as jnp
from jax import lax
from jax.experimental import pallas as pl
from jax.experimental.pallas import tpu as pltpu

F32 = jnp.float32
BF16 = jnp.bfloat16

N_HEADS = 8
HEAD_DIM = 64
V_DIM = 2 * HEAD_DIM
D_MODEL = N_HEADS * V_DIM
SSM_GROUP = 16
SSM_GROUPS = D_MODEL // SSM_GROUP
SSM_STATE = 64
CONV_W = 3
PAGE_SIZE = 128
LN_EPS = 1e-5
QK_SCALE = HEAD_DIM ** -0.5

LANES = 128
MXU_DIM = 256
VMEM_LIMIT = 56 * 1024 * 1024

GROUPS_PER_BLOCK = MXU_DIM // SSM_GROUP
STATE_BLOCK = GROUPS_PER_BLOCK * SSM_STATE
N_GROUP_BLOCKS = SSM_GROUPS // GROUPS_PER_BLOCK
FF_CHUNK = MXU_DIM


def _params(*sem):
    return pltpu.CompilerParams(dimension_semantics=sem, vmem_limit_bytes=VMEM_LIMIT)


def _full(shape):
    return pl.BlockSpec(shape, lambda *_: (0,) * len(shape))


def _gelu(x):
    return x * (0.5 * (1.0 + jnp.tanh(math.sqrt(2.0 / math.pi) * (x + 0.044715 * (x * x * x)))))


def _sigmoid(x):
    return 1.0 / (1.0 + jnp.exp(-x))


def _layer_norm(x, g, b):
    mu = jnp.mean(x, -1, keepdims=True)
    xc = x - mu
    var = jnp.mean(xc * xc, -1, keepdims=True)
    return xc * lax.rsqrt(var + LN_EPS) * g + b


def _lambda(lq1, lk1, lq2, lk2, lam_init):
    return (jnp.exp(jnp.sum(lq1[...] * lk1[...], keepdims=True))
            - jnp.exp(jnp.sum(lq2[...] * lk2[...], keepdims=True)) + lam_init)


def _inproj_kernel(x_ref, w_ref, q_ref, qt_ref, kf_ref, kb_ref, vf_ref, vt_ref, u_ref, ga_ref, gs_ref):
    x = x_ref[...].astype(BF16)

    def section(j):
        return jnp.dot(x, w_ref[:, j * D_MODEL:(j + 1) * D_MODEL], preferred_element_type=F32)

    q = section(0) * QK_SCALE
    q_ref[...] = q.astype(BF16)
    qt_ref[...] = q.T.astype(BF16)
    k = section(1)
    kf_ref[...] = k
    kb_ref[...] = k.astype(BF16)
    v = section(2)
    vf_ref[...] = v
    vt_ref[...] = v.T.astype(BF16)
    u_ref[...] = section(3)
    ga_ref[...] = section(4)
    gs_ref[...] = section(5)


def _inproj(x, w_in, tm):
    nb, seq, d = x.shape
    tmaj = pl.BlockSpec((tm, d), lambda b, i: (i, b))
    bmaj = pl.BlockSpec((None, tm, d), lambda b, i: (b, i, 0))
    t_shape = lambda dt: jax.ShapeDtypeStruct((seq, nb * d), dt)
    b_shape = jax.ShapeDtypeStruct((nb, seq, d), F32)
    return pl.pallas_call(
        _inproj_kernel,
        grid=(nb, seq // tm),
        in_specs=[bmaj, _full(w_in.shape)],
        out_specs=[tmaj, pl.BlockSpec((d, tm), lambda b, i: (b, i)), bmaj, tmaj, bmaj,
                   pl.BlockSpec((None, d, tm), lambda b, i: (i, b, 0)), tmaj, tmaj, tmaj],
        out_shape=[t_shape(BF16), jax.ShapeDtypeStruct((nb * d, seq), BF16), b_shape, t_shape(BF16), b_shape,
                   jax.ShapeDtypeStruct((seq // tm, nb * d, tm), BF16),
                   t_shape(F32), t_shape(F32), t_shape(F32)],
        compiler_params=_params("parallel", "parallel"),
        name="inproj",
    )(x, w_in)


def _attn_prompt_kernel(lq1, lk1, lq2, lk2, g_ref, qt_ref, k_ref, vt_ref, o_ref,
                        qq_sc, m_sc, l_sc, acc_sc, *, tq, heads, lam_init):
    i = pl.program_id(2)
    lam = _lambda(lq1, lk1, lq2, lk2, lam_init)
    head_rows = lambda hh: slice(hh * V_DIM, (hh + 1) * V_DIM)
    for hh in range(heads):
        qt = qt_ref[head_rows(hh), :]
        sub = lax.broadcasted_iota(jnp.int32, qt.shape, 0)
        zero = jnp.zeros_like(qt)
        qq_sc[hh] = jnp.concatenate([jnp.where(sub < HEAD_DIM, qt, zero),
                                     jnp.where(sub >= HEAD_DIM, qt, zero)], axis=1)
    m_sc[...] = jnp.full(m_sc.shape, -jnp.inf, F32)
    l_sc[...] = jnp.zeros(l_sc.shape, F32)
    acc_sc[...] = jnp.zeros(acc_sc.shape, F32)

    def update(j, masked):
        start = pl.multiple_of(j * tq, tq)
        for hh in range(heads):
            kk = k_ref[pl.ds(start, tq), head_rows(hh)]
            s = jnp.dot(kk, qq_sc[hh], preferred_element_type=F32)
            if masked:
                row = lax.broadcasted_iota(jnp.int32, s.shape, 0)
                col = lax.broadcasted_iota(jnp.int32, s.shape, 1)
                qpos = jnp.where(col >= tq, col - tq, col)
                s = jnp.where(row <= qpos, s, -jnp.inf)
            m_prev = m_sc[hh]
            m_new = jnp.maximum(m_prev, jnp.max(s, 0, keepdims=True))
            corr = jnp.exp(m_prev - m_new)
            p = jnp.exp(s - m_new)
            l_sc[hh] = l_sc[hh] * corr + jnp.sum(p, 0, keepdims=True)
            acc_sc[hh] = acc_sc[hh] * corr + jnp.dot(vt_ref[j, head_rows(hh), :], p.astype(BF16),
                                                      preferred_element_type=F32)
            m_sc[hh] = m_new

    def body(j, carry):
        update(j, False)
        return carry

    lax.fori_loop(0, i, body, 0)
    update(i, True)
    for hh in range(heads):
        out = acc_sc[hh] / l_sc[hh]
        o = out[:, :tq] - lam * out[:, tq:]
        ms = jnp.mean(o * o, 0, keepdims=True)
        o_ref[:, head_rows(hh)] = (o * lax.rsqrt(ms + LN_EPS) * g_ref[...] * (1.0 - lam_init)).T


def _attn_prompt(qt, k, vt, lams, subln_g, nb, lam_init, heads):
    tq = vt.shape[-1]
    seq = k.shape[0]
    width = heads * V_DIM
    group = lambda b, h: b * (N_HEADS // heads) + h
    vec = _full((1, HEAD_DIM))
    return pl.pallas_call(
        functools.partial(_attn_prompt_kernel, tq=tq, heads=heads, lam_init=lam_init),
        grid=(nb, N_HEADS // heads, seq // tq),
        in_specs=[vec, vec, vec, vec, _full((V_DIM, 1)),
                  pl.BlockSpec((width, tq), lambda b, h, i: (group(b, h), i)),
                  pl.BlockSpec((seq, width), lambda b, h, i: (0, group(b, h))),
                  pl.BlockSpec((seq // tq, width, tq), lambda b, h, i: (0, group(b, h), 0))],
        out_specs=pl.BlockSpec((tq, width), lambda b, h, i: (i, group(b, h))),
        out_shape=jax.ShapeDtypeStruct(k.shape, F32),
        scratch_shapes=[pltpu.VMEM((heads, V_DIM, 2 * tq), BF16), pltpu.VMEM((heads, 1, 2 * tq), F32),
                        pltpu.VMEM((heads, 1, 2 * tq), F32), pltpu.VMEM((heads, V_DIM, 2 * tq), F32)],
        compiler_params=_params("parallel", "parallel", "arbitrary"),
        name="attn_prompt",
    )(*lams, subln_g.reshape(V_DIM, 1), qt, k, vt)


ROWS = N_HEADS * 2 * 4


def _attn_sample_kernel(pt_ref, q8_ref, *refs, pages_per_step, lam_init):
    del pt_ref
    k_refs = refs[:pages_per_step]
    v_refs = refs[pages_per_step:2 * pages_per_step]
    (kn_ref, vn_ref, lq1, lk1, lq2, lk2, g_ref, o_ref,
     qbd_sc, m_sc, l_sc, acc_sc) = refs[2 * pages_per_step:]
    n = pl.program_id(1)

    @pl.when(n == 0)
    def _():
        qt = jnp.concatenate([q8_ref[...]] * N_HEADS, axis=0)
        row = lax.broadcasted_iota(jnp.int32, qt.shape, 0)
        col = lax.broadcasted_iota(jnp.int32, qt.shape, 1)
        qbd_sc[...] = jnp.where((col >> 6) == (row >> 2), qt, 0.0).astype(BF16)
        m_sc[...] = jnp.full(m_sc.shape, -jnp.inf, F32)
        l_sc[...] = jnp.zeros(l_sc.shape, F32)
        acc_sc[...] = jnp.zeros(acc_sc.shape, F32)

    pair_rows = 2 * 8
    n_pairs = N_HEADS // 2
    nt = (((1,), (1,)), ((), ()))

    def head_pair(ref, hp):
        a = ref[pl.ds(2 * hp, PAGE_SIZE, stride=N_HEADS), :]
        b = ref[pl.ds(2 * hp + 1, PAGE_SIZE, stride=N_HEADS), :]
        return jnp.concatenate([a, b], axis=1).astype(BF16)

    def page_scores(k_ref):
        parts = []
        for hp in range(n_pairs):
            qp = qbd_sc[hp * pair_rows:(hp + 1) * pair_rows, hp * 2 * V_DIM:(hp + 1) * 2 * V_DIM]
            parts.append(lax.dot_general(qp, head_pair(k_ref, hp), nt, preferred_element_type=F32))
        return jnp.concatenate(parts, axis=0)

    def page_values(p, v_ref):
        return [jnp.dot(p[hp * pair_rows:(hp + 1) * pair_rows], head_pair(v_ref, hp),
                        preferred_element_type=F32) for hp in range(n_pairs)]

    def pair_diag(pv):
        return jnp.concatenate(
            [blk for d in pv for blk in (d[:8, :V_DIM], d[8:, V_DIM:])], axis=0)

    def head_diag(pv):
        return jnp.concatenate(
            [pv[h * 8:(h + 1) * 8, h * V_DIM:(h + 1) * V_DIM] for h in range(N_HEADS)], axis=0)

    def update(s_list, values):
        m_prev = m_sc[...]
        m_new = m_prev
        for s in s_list:
            m_new = jnp.maximum(m_new, jnp.max(s, -1, keepdims=True))
        corr = jnp.exp(m_prev - m_new)
        l_new = l_sc[...] * corr
        p_list = []
        for s in s_list:
            p = jnp.exp(s - m_new)
            l_new = l_new + jnp.sum(p, -1, keepdims=True)
            p_list.append(p.astype(BF16))
        l_sc[...] = l_new
        acc_sc[...] = acc_sc[...] * corr + values(p_list)
        m_sc[...] = m_new

    def cached_values(p_list):
        total = None
        for p, v_ref in zip(p_list, v_refs):
            pv = page_values(p, v_ref)
            total = pv if total is None else [a + b for a, b in zip(total, pv)]
        return pair_diag(total)

    update([page_scores(k_ref) for k_ref in k_refs], cached_values)

    @pl.when(n == pl.num_programs(1) - 1)
    def _():
        s = lax.dot_general(qbd_sc[...], kn_ref[...], nt, preferred_element_type=F32)
        row = lax.broadcasted_iota(jnp.int32, s.shape, 0)
        col = lax.broadcasted_iota(jnp.int32, s.shape, 1)
        s = jnp.where(col <= (row & 3), s, -jnp.inf)
        update([s], lambda p_list: head_diag(
            jnp.dot(p_list[0], vn_ref[...], preferred_element_type=F32)))
        lam = _lambda(lq1, lk1, lq2, lk2, lam_init)
        out = acc_sc[...] / l_sc[...]
        for h in range(N_HEADS):
            oh = out[h * 8:(h + 1) * 8]
            o = oh - lam * pltpu.roll(oh, 4, axis=0)
            ms = jnp.mean(o * o, -1, keepdims=True)
            o_ref[:, h * V_DIM:(h + 1) * V_DIM] = (
                o * lax.rsqrt(ms + LN_EPS) * g_ref[...] * (1.0 - lam_init))


def _attn_sample(q8, k_new, v_new, cache_k, cache_v, page_table, lams, subln_g, lam_init, pages_per_step):
    nbd, n_pages = page_table.shape
    width = q8.shape[-1]

    def page_spec(p):
        return pl.BlockSpec((None, PAGE_SIZE * N_HEADS, V_DIM),
                            lambda b, n, pt: (pt[b, n * pages_per_step + p], 0, 0))

    per_b = lambda rows: pl.BlockSpec((None, rows, width), lambda b, n, pt: (b, 0, 0))
    vec = pl.BlockSpec((1, HEAD_DIM), lambda b, n, pt: (0, 0))
    grid_spec = pltpu.PrefetchScalarGridSpec(
        num_scalar_prefetch=1,
        grid=(nbd, n_pages // pages_per_step),
        in_specs=([per_b(8)] + [page_spec(p) for p in range(pages_per_step)] * 2
                  + [per_b(k_new.shape[1]), per_b(v_new.shape[1]), vec, vec, vec, vec,
                     pl.BlockSpec((1, V_DIM), lambda b, n, pt: (0, 0))]),
        out_specs=per_b(8),
        scratch_shapes=[pltpu.VMEM((ROWS, width), BF16), pltpu.VMEM((ROWS, 1), F32),
                        pltpu.VMEM((ROWS, 1), F32), pltpu.VMEM((ROWS, V_DIM), F32)],
    )
    return pl.pallas_call(
        functools.partial(_attn_sample_kernel, pages_per_step=pages_per_step, lam_init=lam_init),
        grid_spec=grid_spec,
        out_shape=jax.ShapeDtypeStruct((nbd, 8, width), F32),
        compiler_params=_params("parallel", "arbitrary"),
        name="attn_sample",
    )(page_table, q8, *([cache_k] * pages_per_step), *([cache_v] * pages_per_step),
      k_new, v_new, *lams, subln_g)


def _discretize_kernel(ar_ref, ai_ref, ldt_ref, br_ref, bi_ref, lbr_ref, lbi_ref, bbr_ref, bbi_ref):
    dt = jnp.exp(ldt_ref[...])
    ar, ai = ar_ref[...], ai_ref[...]
    mag = jnp.exp(ar * dt)
    lb_re, lb_im = mag * jnp.cos(ai * dt), mag * jnp.sin(ai * dt)
    den = ar * ar + ai * ai
    nr, ni = lb_re - 1.0, lb_im
    cr = (nr * ar + ni * ai) / den
    ci = (ni * ar - nr * ai) / den
    br, bi = br_ref[...], bi_ref[...]
    lbr_ref[...] = lb_re
    lbi_ref[...] = lb_im
    bbr_ref[...] = cr * br - ci * bi
    bbi_ref[...] = cr * bi + ci * br


def _discretize(a_re, a_im, log_dt, b_re, b_im):
    rep = lambda a: jnp.repeat(a, SSM_GROUP, axis=0)
    to_rows = lambda b: jnp.transpose(b, (0, 2, 1)).reshape(SSM_GROUPS * SSM_GROUP, SSM_STATE)
    shape = jax.ShapeDtypeStruct((SSM_GROUPS * SSM_GROUP, SSM_STATE), F32)
    lbr, lbi, bbr, bbi = pl.pallas_call(
        _discretize_kernel, out_shape=[shape] * 4, name="ssm_discretize",
    )(rep(a_re), rep(a_im), rep(log_dt[:, None]), to_rows(b_re), to_rows(b_im))
    unrows = lambda b: b.reshape(SSM_GROUPS, SSM_GROUP, SSM_STATE)
    return lbr[::SSM_GROUP], lbi[::SSM_GROUP], unrows(bbr), unrows(bbi)


def _block_diag_weights(bb_re, bb_im, c_re, c_im):
    eye = jnp.eye(GROUPS_PER_BLOCK, dtype=F32)

    def in_w(bb):
        bb = bb.reshape(N_GROUP_BLOCKS, GROUPS_PER_BLOCK, SSM_GROUP, SSM_STATE)
        w = bb[:, :, :, None, :] * eye[None, :, None, :, None]
        return w.reshape(N_GROUP_BLOCKS, MXU_DIM, STATE_BLOCK)

    def out_w(c):
        c = c.reshape(N_GROUP_BLOCKS, GROUPS_PER_BLOCK, SSM_GROUP, SSM_STATE)
        w = jnp.transpose(c, (0, 1, 3, 2))[:, :, :, None, :] * eye[None, :, None, :, None]
        return w.reshape(N_GROUP_BLOCKS, STATE_BLOCK, MXU_DIM)

    w_b = jnp.concatenate([in_w(bb_re), in_w(bb_im)], axis=2).astype(BF16)
    w_c = jnp.concatenate([out_w(c_re), out_w(-c_im)], axis=1).astype(BF16)
    return w_b, w_c


def _ssm_kernel(u_ref, h0r_ref, h0i_ref, lbr_ref, lbi_ref, wb_ref, wc_ref, d_ref,
                y_ref, hr_ref, hi_ref, hs_sc, sr_sc, si_sc, *, nb, steps, unroll):
    j = pl.program_id(1)

    @pl.when(j == 0)
    def _():
        sr_sc[...] = h0r_ref[...]
        si_sc[...] = h0i_ref[...]

    u = u_ref[...]
    hs_sc[...] = jnp.dot(u.astype(BF16), wb_ref[...], preferred_element_type=F32)
    lbr, lbi = lbr_ref[...], lbi_ref[...]

    def step(t, carry):
        hr, hi = carry
        r0 = pl.multiple_of(t * nb, nb)
        bur = hs_sc[pl.ds(r0, nb), :STATE_BLOCK]
        bui = hs_sc[pl.ds(r0, nb), STATE_BLOCK:]
        nr = lbr * hr - lbi * hi + bur
        ni = lbr * hi + lbi * hr + bui
        hs_sc[pl.ds(r0, nb), :STATE_BLOCK] = nr
        hs_sc[pl.ds(r0, nb), STATE_BLOCK:] = ni
        return nr, ni

    hr, hi = lax.fori_loop(0, steps, step, (sr_sc[...], si_sc[...]), unroll=unroll)
    sr_sc[...] = hr
    si_sc[...] = hi
    y_ref[...] = (jnp.dot(hs_sc[...].astype(BF16), wc_ref[...], preferred_element_type=F32)
                  + d_ref[...] * u)

    @pl.when(j == pl.num_programs(1) - 1)
    def _():
        hr_ref[...] = hr
        hi_ref[...] = hi


def _ssm(u, h0_re, h0_im, lb_re, lb_im, w_b, w_c, d_skip, nb, steps):
    rows = u.shape[0]
    tile = steps * nb
    ublk = pl.BlockSpec((tile, MXU_DIM), lambda g, j: (j, g))
    sblk = pl.BlockSpec((nb, STATE_BLOCK), lambda g, j: (0, g))
    lblk = pl.BlockSpec((1, STATE_BLOCK), lambda g, j: (0, g))
    state = jax.ShapeDtypeStruct(h0_re.shape, F32)
    return pl.pallas_call(
        functools.partial(_ssm_kernel, nb=nb, steps=steps, unroll=(4 if nb <= 8 else True)),
        grid=(N_GROUP_BLOCKS, rows // tile),
        in_specs=[ublk, sblk, sblk, lblk, lblk,
                  pl.BlockSpec((None, MXU_DIM, 2 * STATE_BLOCK), lambda g, j: (g, 0, 0)),
                  pl.BlockSpec((None, 2 * STATE_BLOCK, MXU_DIM), lambda g, j: (g, 0, 0)),
                  pl.BlockSpec((1, MXU_DIM), lambda g, j: (0, g))],
        out_specs=[ublk, sblk, sblk],
        out_shape=[jax.ShapeDtypeStruct(u.shape, F32), state, state],
        scratch_shapes=[pltpu.VMEM((tile, 2 * STATE_BLOCK), F32),
                        pltpu.VMEM((nb, STATE_BLOCK), F32), pltpu.VMEM((nb, STATE_BLOCK), F32)],
        compiler_params=_params("parallel", "arbitrary"),
        name="ssm_scan",
    )(u, h0_re, h0_im, lb_re, lb_im, w_b, w_c, d_skip)


def _merge_kernel(x_ref, o_ref, ga_ref, gs_ref, y_ref, wg_ref, bg_ref, wo_ref, g1_ref, b1_ref,
                  x1_ref, *, alpha):
    gl = _gelu(y_ref[...])
    z = jnp.dot(gl.astype(BF16), wg_ref[...], preferred_element_type=F32) + bg_ref[...]
    s = gl * _sigmoid(z)
    merged = _sigmoid(ga_ref[...]) * o_ref[...] + _sigmoid(gs_ref[...]) * s
    r = alpha * x_ref[...] + jnp.dot(merged.astype(BF16), wo_ref[...], preferred_element_type=F32)
    x1_ref[...] = _layer_norm(r, g1_ref[...], b1_ref[...])


def _merge(x, o, ga, gs, y, w_glu, b_glu, w_out, ln_g, ln_b, alpha, tm):
    nb, seq, d = x.shape
    tmaj = pl.BlockSpec((tm, d), lambda b, i: (i, b))
    bmaj = pl.BlockSpec((None, tm, d), lambda b, i: (b, i, 0))
    return pl.pallas_call(
        functools.partial(_merge_kernel, alpha=alpha),
        grid=(nb, seq // tm),
        in_specs=[bmaj, tmaj, tmaj, tmaj, tmaj, _full((d, d)), _full((1, d)), _full((d, d)),
                  _full((1, d)), _full((1, d))],
        out_specs=tmaj,
        out_shape=jax.ShapeDtypeStruct((seq, nb * d), F32),
        compiler_params=_params("parallel", "parallel"),
        name="merge_ln1",
    )(x, o, ga, gs, y, w_glu, b_glu, w_out, ln_g, ln_b)


def _ffn_kernel(x1_ref, buf_ref, wup_ref, cw_ref, cb_ref, wdn_ref, g2_ref, b2_ref,
                x2_ref, nbuf_ref, carry_sc, *, nb, alpha, d_ff):
    i = pl.program_id(0)
    rows = x1_ref.shape[0]

    @pl.when(i == 0)
    def _():
        carry_sc[...] = buf_ref[...]

    x1 = x1_ref[...]
    xb = x1.astype(BF16)
    f = None
    for c in range(d_ff // FF_CHUNK):
        lo, hi = c * FF_CHUNK, (c + 1) * FF_CHUNK
        a = jnp.dot(xb, wup_ref[:, lo:hi], preferred_element_type=F32)
        gate = jnp.dot(xb, wup_ref[:, d_ff + lo:d_ff + hi], preferred_element_type=F32)
        ap = jnp.concatenate([carry_sc[:, lo:hi], a], axis=0)
        conv = (cb_ref[:, lo:hi] + ap[:rows] * cw_ref[0:1, lo:hi]
                + ap[nb:rows + nb] * cw_ref[1:2, lo:hi] + ap[2 * nb:] * cw_ref[2:3, lo:hi])
        carry_sc[:, lo:hi] = ap[rows:]
        h = _gelu(conv) * gate
        d = jnp.dot(h.astype(BF16), wdn_ref[lo:hi, :], preferred_element_type=F32)
        f = d if f is None else f + d
    x2_ref[...] = _layer_norm(alpha * x1 + f, g2_ref[...], b2_ref[...])

    @pl.when(i == pl.num_programs(0) - 1)
    def _():
        nbuf_ref[...] = carry_sc[...]


def _ffn(x1, buf, w_up, conv_w, conv_b, w_down, ln_g, ln_b, nb, alpha, tile):
    rows, d = x1.shape
    d_ff = w_down.shape[0]
    xblk = pl.BlockSpec((tile, d), lambda i: (i, 0))
    return pl.pallas_call(
        functools.partial(_ffn_kernel, nb=nb, alpha=alpha, d_ff=d_ff),
        grid=(rows // tile,),
        in_specs=[xblk, _full(buf.shape), _full(w_up.shape), _full(conv_w.shape), _full((1, d_ff)),
                  _full(w_down.shape), _full((1, d)), _full((1, d))],
        out_specs=[xblk, _full(buf.shape)],
        out_shape=[jax.ShapeDtypeStruct(x1.shape, F32), jax.ShapeDtypeStruct(buf.shape, F32)],
        scratch_shapes=[pltpu.VMEM(buf.shape, F32)],
        compiler_params=_params("arbitrary"),
        name="convffn_ln2",
    )(x1, buf, w_up, conv_w, conv_b, w_down, ln_g, ln_b)


def _layer_forward(x, nb, seq, attend, h_re, h_im, conv_buf, alpha, w, *, row_tile, ssm_steps):
    d = x.shape[-1]
    d_ff = w["w_down"].shape[0]
    q, qt, k_f32, k, v_f32, vt, u, ga, gs = _inproj(x, w["w_in"], row_tile // 2)

    o = attend(q, qt, k, v_f32, vt)

    y, hr, hi = _ssm(u.reshape(seq * nb, d), h_re.reshape(nb, -1), h_im.reshape(nb, -1),
                     w["lb_re"], w["lb_im"], w["ssm_w_b"], w["ssm_w_c"], w["ssm_d"], nb, ssm_steps)

    x1 = _merge(x, o, ga, gs, y.reshape(q.shape), w["w_glu"], w["b_glu"], w["w_out"],
                w["ln1_g"], w["ln1_b"], alpha, row_tile)

    buf = jnp.transpose(conv_buf, (1, 0, 2)).reshape((CONV_W - 1) * nb, d_ff)
    x2, nbuf = _ffn(x1.reshape(seq * nb, d), buf, w["w_up"], w["conv_w"], w["conv_b"], w["w_down"],
                    w["ln2_g"], w["ln2_b"], nb, alpha, row_tile)

    nbuf = jnp.transpose(nbuf.reshape(CONV_W - 1, nb, d_ff), (1, 0, 2))
    shape = (nb, SSM_GROUPS, SSM_STATE)
    return x2, k_f32, v_f32, hr.reshape(shape), hi.reshape(shape), nbuf


def kernel(x_prompt, x_sample, cache_k, cache_v, state_ssm_re, state_ssm_im, state_conv, page_table, w_in, lambda_q1, lambda_k1, lambda_q2, lambda_k2, subln_g, ssm_a_re, ssm_a_im, ssm_log_dt, ssm_b_re, ssm_b_im, ssm_c_re, ssm_c_im, ssm_d, w_glu, b_glu, w_out, ln1_g, ln1_b, w_up, conv_w, conv_b, w_down, ln2_g, ln2_b):
    depth = w_in.shape[0]
    alpha = (2.0 * depth) ** 0.25
    bp, seq, d = x_prompt.shape
    bd, dec_seq, _ = x_sample.shape
    d_ff = w_down.shape[1]
    row_tile = 512
    assert dec_seq == 4 and seq % row_tile == 0 and (bd * dec_seq) % row_tile == 0

    yp = x_prompt
    ys = jnp.transpose(x_sample, (1, 0, 2)).reshape(1, dec_seq * bd, d)
    outs = {name: [] for name in ("kp", "vp", "ks", "vs", "hrp", "hip", "hrs", "his", "cp", "cs")}
    for layer in range(depth):
        lam_init = 0.8 - 0.6 * math.exp(-0.3 * layer)
        row = lambda a: a[layer].reshape(1, -1)
        lb_re, lb_im, bb_re, bb_im = _discretize(ssm_a_re[layer], ssm_a_im[layer], ssm_log_dt[layer],
                                                  ssm_b_re[layer], ssm_b_im[layer])
        ssm_w_b, ssm_w_c = _block_diag_weights(bb_re, bb_im, ssm_c_re[layer], ssm_c_im[layer])
        w = dict(
            w_in=w_in[layer].astype(BF16), lb_re=lb_re.reshape(1, -1), lb_im=lb_im.reshape(1, -1),
            ssm_w_b=ssm_w_b, ssm_w_c=ssm_w_c, ssm_d=row(ssm_d),
            w_glu=w_glu[layer].astype(BF16), b_glu=row(b_glu), w_out=w_out[layer].astype(BF16),
            ln1_g=row(ln1_g), ln1_b=row(ln1_b), w_up=w_up[layer].astype(BF16),
            conv_w=conv_w[layer], conv_b=row(conv_b), w_down=w_down[layer].astype(BF16),
            ln2_g=row(ln2_g), ln2_b=row(ln2_b))
        lams = (row(lambda_q1), row(lambda_k1), row(lambda_q2), row(lambda_k2))
        g_sub = row(subln_g)

        attend_p = lambda q, qt, k, v, vt: _attn_prompt(qt, k, vt, lams, g_sub, bp, lam_init, 2)
        zeros_h = jnp.zeros((bp, SSM_GROUPS, SSM_STATE), F32)
        yp, kp, vp, hrp, hip, cbp = _layer_forward(
            yp, bp, seq, attend_p, zeros_h, zeros_h, jnp.zeros((bp, CONV_W - 1, d_ff), F32), alpha, w,
            row_tile=row_tile, ssm_steps=128)
        yp = jnp.transpose(yp.reshape(seq, bp, d), (1, 0, 2))

        n_pool = cache_k.shape[1]
        as_pages = lambda c: c.reshape(depth * n_pool, PAGE_SIZE * N_HEADS, V_DIM)
        pages = page_table + layer * n_pool
        per_seq = lambda a: jnp.transpose(a.reshape(dec_seq, bd, d), (1, 0, 2))

        def attend_s(q, qt, k, v, vt):
            qs = per_seq(q).astype(F32)
            q8 = jnp.concatenate([qs, qs], axis=1)
            pad = lambda a: jnp.pad(per_seq(a).astype(BF16), ((0, 0), (0, 16 - dec_seq), (0, 0)))
            o8 = _attn_sample(q8, pad(k), pad(v), as_pages(cache_k), as_pages(cache_v), pages,
                              lams, g_sub, lam_init, 8)
            return jnp.transpose(o8[:, :dec_seq], (1, 0, 2)).reshape(dec_seq * bd, d)

        ys, ks, vs, hrs, his, cbs = _layer_forward(
            ys, bd, dec_seq, attend_s, state_ssm_re[layer], state_ssm_im[layer], state_conv[layer],
            alpha, w, row_tile=row_tile, ssm_steps=dec_seq)
        ys = ys.reshape(1, dec_seq * bd, d)
        ks, vs = per_seq(ks), per_seq(vs)
        for name, val in zip(("kp", "vp", "ks", "vs", "hrp", "hip", "hrs", "his", "cp", "cs"),
                             (kp, vp, ks, vs, hrp, hip, hrs, his, cbp, cbs)):
            outs[name].append(val)

    y_sample = per_seq(ys)
    st = lambda name: jnp.stack(outs[name])
    heads = lambda a: a.reshape(a.shape[:-1] + (N_HEADS, V_DIM))
    return (yp, y_sample, heads(st("kp")), heads(st("vp")), heads(st("ks")), heads(st("vs")),
            st("hrp"), st("hip"), st("hrs"), st("his"), st("cp"), st("cs"))
```

```python
import functools
import math

import jax
import jax.numpy as jnp
from jax import lax
from jax.experimental import pallas as pl
from jax.experimental.pallas import tpu as pltpu

F32 = jnp.float32
BF16 = jnp.bfloat16

N_HEADS = 8
HEAD_DIM = 64
V_DIM = 2 * HEAD_DIM
D_MODEL = N_HEADS * V_DIM
SSM_GROUP = 16
SSM_GROUPS = D_MODEL // SSM_GROUP
SSM_STATE = 64
CONV_W = 3
PAGE_SIZE = 128
LN_EPS = 1e-5
QK_SCALE = HEAD_DIM ** -0.5

LANES = 128
MXU_DIM = 256
VMEM_LIMIT = 56 * 1024 * 1024

GROUPS_PER_BLOCK = MXU_DIM // SSM_GROUP
STATE_BLOCK = GROUPS_PER_BLOCK * SSM_STATE
N_GROUP_BLOCKS = SSM_GROUPS // GROUPS_PER_BLOCK
FF_CHUNK = MXU_DIM


def _params(*sem):
    return pltpu.CompilerParams(dimension_semantics=sem, vmem_limit_bytes=VMEM_LIMIT)


def _full(shape):
    return pl.BlockSpec(shape, lambda *_: (0,) * len(shape))


def _gelu(x):
    return x * (0.5 * (1.0 + jnp.tanh(math.sqrt(2.0 / math.pi) * (x + 0.044715 * (x * x * x)))))


def _sigmoid(x):
    return 1.0 / (1.0 + jnp.exp(-x))


def _layer_norm(x, g, b):
    mu = jnp.mean(x, -1, keepdims=True)
    xc = x - mu
    var = jnp.mean(xc * xc, -1, keepdims=True)
    return xc * lax.rsqrt(var + LN_EPS) * g + b


def _lambda(lq1, lk1, lq2, lk2, lam_init):
    return (jnp.exp(jnp.sum(lq1[...] * lk1[...], keepdims=True))
            - jnp.exp(jnp.sum(lq2[...] * lk2[...], keepdims=True)) + lam_init)


def _inproj_kernel(x_ref, w_ref, q_ref, qt_ref, kf_ref, kb_ref, vf_ref, vt_ref, u_ref, ga_ref, gs_ref):
    x = x_ref[...].astype(BF16)

    def section(j):
        return jnp.dot(x, w_ref[:, j * D_MODEL:(j + 1) * D_MODEL], preferred_element_type=F32)

    q = section(0) * QK_SCALE
    q_ref[...] = q.astype(BF16)
    qt_ref[...] = q.T.astype(BF16)
    k = section(1)
    kf_ref[...] = k
    kb_ref[...] = k.astype(BF16)
    v = section(2)
    vf_ref[...] = v
    vt_ref[...] = v.T.astype(BF16)
    u_ref[...] = section(3)
    ga_ref[...] = section(4)
    gs_ref[...] = section(5)


def _inproj(x, w_in, tm):
    nb, seq, d = x.shape
    tmaj = pl.BlockSpec((tm, d), lambda b, i: (i, b))
    bmaj = pl.BlockSpec((None, tm, d), lambda b, i: (b, i, 0))
    t_shape = lambda dt: jax.ShapeDtypeStruct((seq, nb * d), dt)
    b_shape = jax.ShapeDtypeStruct((nb, seq, d), F32)
    return pl.pallas_call(
        _inproj_kernel,
        grid=(nb, seq // tm),
        in_specs=[bmaj, _full(w_in.shape)],
        out_specs=[tmaj, pl.BlockSpec((d, tm), lambda b, i: (b, i)), bmaj, tmaj, bmaj,
                   pl.BlockSpec((None, d, tm), lambda b, i: (i, b, 0)), tmaj, tmaj, tmaj],
        out_shape=[t_shape(BF16), jax.ShapeDtypeStruct((nb * d, seq), BF16), b_shape, t_shape(BF16), b_shape,
                   jax.ShapeDtypeStruct((seq // tm, nb * d, tm), BF16),
                   t_shape(F32), t_shape(F32), t_shape(F32)],
        compiler_params=_params("parallel", "parallel"),
        name="inproj",
    )(x, w_in)


def _attn_prompt_kernel(lq1, lk1, lq2, lk2, g_ref, qt_ref, k_ref, vt_ref, o_ref,
                        qq_sc, s_sc, m_sc, l_sc, acc_sc, *, tq, heads, lam_init):
    i = pl.program_id(2)
    lam = _lambda(lq1, lk1, lq2, lk2, lam_init)
    head_rows = lambda hh: slice(hh * V_DIM, (hh + 1) * V_DIM)
    for hh in range(heads):
        qt = qt_ref[head_rows(hh), :]
        sub = lax.broadcasted_iota(jnp.int32, qt.shape, 0)
        zero = jnp.zeros_like(qt)
        qq_sc[hh] = jnp.concatenate([jnp.where(sub < HEAD_DIM, qt, zero),
                                     jnp.where(sub >= HEAD_DIM, qt, zero)], axis=1)
    m_sc[...] = jnp.full(m_sc.shape, -jnp.inf, F32)
    l_sc[...] = jnp.zeros(l_sc.shape, F32)
    acc_sc[...] = jnp.zeros(acc_sc.shape, F32)

    def scores(j, hh):
        start = pl.multiple_of(j * tq, tq)
        kk = k_ref[pl.ds(start, tq), head_rows(hh)]
        return jnp.dot(kk, qq_sc[hh], preferred_element_type=F32)

    for hh in range(heads):
        s_sc[hh] = scores(0, hh)

    def update(j, masked):
        for hh in range(heads):
            s = s_sc[hh]
            if not masked:
                s_sc[hh] = scores(j + 1, hh)
            if masked:
                row = lax.broadcasted_iota(jnp.int32, s.shape, 0)
                col = lax.broadcasted_iota(jnp.int32, s.shape, 1)
                qpos = jnp.where(col >= tq, col - tq, col)
                s = jnp.where(row <= qpos, s, -jnp.inf)
            m_prev = m_sc[hh]
            m_new = jnp.maximum(m_prev, jnp.max(s, 0, keepdims=True))
            corr = jnp.exp(m_prev - m_new)
            p = jnp.exp(s - m_new)
            l_sc[hh] = l_sc[hh] * corr + jnp.sum(p, 0, keepdims=True)
            acc_sc[hh] = acc_sc[hh] * corr + jnp.dot(vt_ref[j, head_rows(hh), :], p.astype(BF16),
                                                      preferred_element_type=F32)
            m_sc[hh] = m_new

    def body(j, carry):
        update(j, False)
        return carry

    lax.fori_loop(0, i, body, 0)
    update(i, True)
    for hh in range(heads):
        out = acc_sc[hh] / l_sc[hh]
        o = out[:, :tq] - lam * out[:, tq:]
        ms = jnp.mean(o * o, 0, keepdims=True)
        o_ref[:, head_rows(hh)] = (o * lax.rsqrt(ms + LN_EPS) * g_ref[...] * (1.0 - lam_init)).T


def _attn_prompt(qt, k, vt, lams, subln_g, nb, lam_init, heads):
    tq = vt.shape[-1]
    seq = k.shape[0]
    width = heads * V_DIM
    group = lambda b, h: b * (N_HEADS // heads) + h
    vec = _full((1, HEAD_DIM))
    return pl.pallas_call(
        functools.partial(_attn_prompt_kernel, tq=tq, heads=heads, lam_init=lam_init),
        grid=(nb, N_HEADS // heads, seq // tq),
        in_specs=[vec, vec, vec, vec, _full((V_DIM, 1)),
                  pl.BlockSpec((width, tq), lambda b, h, i: (group(b, h), i)),
                  pl.BlockSpec((seq, width), lambda b, h, i: (0, group(b, h))),
                  pl.BlockSpec((seq // tq, width, tq), lambda b, h, i: (0, group(b, h), 0))],
        out_specs=pl.BlockSpec((tq, width), lambda b, h, i: (i, group(b, h))),
        out_shape=jax.ShapeDtypeStruct(k.shape, F32),
        scratch_shapes=[pltpu.VMEM((heads, V_DIM, 2 * tq), BF16), pltpu.VMEM((heads, tq, 2 * tq), F32),
                        pltpu.VMEM((heads, 1, 2 * tq), F32),
                        pltpu.VMEM((heads, 1, 2 * tq), F32), pltpu.VMEM((heads, V_DIM, 2 * tq), F32)],
        compiler_params=_params("parallel", "parallel", "arbitrary"),
        name="attn_prompt",
    )(*lams, subln_g.reshape(V_DIM, 1), qt, k, vt)


ROWS = N_HEADS * 2 * 4


def _attn_sample_kernel(pt_ref, q8_ref, *refs, pages_per_step, lam_init):
    del pt_ref
    k_refs = refs[:pages_per_step]
    v_refs = refs[pages_per_step:2 * pages_per_step]
    (kn_ref, vn_ref, lq1, lk1, lq2, lk2, g_ref, o_ref,
     qbd_sc, m_sc, l_sc, acc_sc) = refs[2 * pages_per_step:]
    n = pl.program_id(1)

    @pl.when(n == 0)
    def _():
        qt = jnp.concatenate([q8_ref[...]] * N_HEADS, axis=0)
        row = lax.broadcasted_iota(jnp.int32, qt.shape, 0)
        col = lax.broadcasted_iota(jnp.int32, qt.shape, 1)
        qbd_sc[...] = jnp.where((col >> 6) == (row >> 2), qt, 0.0).astype(BF16)
        m_sc[...] = jnp.full(m_sc.shape, -jnp.inf, F32)
        l_sc[...] = jnp.zeros(l_sc.shape, F32)
        acc_sc[...] = jnp.zeros(acc_sc.shape, F32)

    pair_rows = 2 * 8
    n_pairs = N_HEADS // 2
    nt = (((1,), (1,)), ((), ()))

    def head_pair(ref, hp):
        a = ref[pl.ds(2 * hp, PAGE_SIZE, stride=N_HEADS), :]
        b = ref[pl.ds(2 * hp + 1, PAGE_SIZE, stride=N_HEADS), :]
        return jnp.concatenate([a, b], axis=1).astype(BF16)

    def page_scores(k_ref):
        parts = []
        for hp in range(n_pairs):
            qp = qbd_sc[hp * pair_rows:(hp + 1) * pair_rows, hp * 2 * V_DIM:(hp + 1) * 2 * V_DIM]
            parts.append(lax.dot_general(qp, head_pair(k_ref, hp), nt, preferred_element_type=F32))
        return jnp.concatenate(parts, axis=0)

    def page_values(p, v_ref):
        return [jnp.dot(p[hp * pair_rows:(hp + 1) * pair_rows], head_pair(v_ref, hp),
                        preferred_element_type=F32) for hp in range(n_pairs)]

    def pair_diag(pv):
        return jnp.concatenate(
            [blk for d in pv for blk in (d[:8, :V_DIM], d[8:, V_DIM:])], axis=0)

    def head_diag(pv):
        return jnp.concatenate(
            [pv[h * 8:(h + 1) * 8, h * V_DIM:(h + 1) * V_DIM] for h in range(N_HEADS)], axis=0)

    def update(s_list, values):
        m_prev = m_sc[...]
        m_new = m_prev
        for s in s_list:
            m_new = jnp.maximum(m_new, jnp.max(s, -1, keepdims=True))
        corr = jnp.exp(m_prev - m_new)
        l_new = l_sc[...] * corr
        p_list = []
        for s in s_list:
            p = jnp.exp(s - m_new)
            l_new = l_new + jnp.sum(p, -1, keepdims=True)
            p_list.append(p.astype(BF16))
        l_sc[...] = l_new
        acc_sc[...] = acc_sc[...] * corr + values(p_list)
        m_sc[...] = m_new

    def cached_values(p_list):
        total = None
        for p, v_ref in zip(p_list, v_refs):
            pv = page_values(p, v_ref)
            total = pv if total is None else [a + b for a, b in zip(total, pv)]
        return pair_diag(total)

    update([page_scores(k_ref) for k_ref in k_refs], cached_values)

    @pl.when(n == pl.num_programs(1) - 1)
    def _():
        s = lax.dot_general(qbd_sc[...], kn_ref[...], nt, preferred_element_type=F32)
        row = lax.broadcasted_iota(jnp.int32, s.shape, 0)
        col = lax.broadcasted_iota(jnp.int32, s.shape, 1)
        s = jnp.where(col <= (row & 3), s, -jnp.inf)
        update([s], lambda p_list: head_diag(
            jnp.dot(p_list[0], vn_ref[...], preferred_element_type=F32)))
        lam = _lambda(lq1, lk1, lq2, lk2, lam_init)
        out = acc_sc[...] / l_sc[...]
        for h in range(N_HEADS):
            oh = out[h * 8:(h + 1) * 8]
            o = oh - lam * pltpu.roll(oh, 4, axis=0)
            ms = jnp.mean(o * o, -1, keepdims=True)
            o_ref[:, h * V_DIM:(h + 1) * V_DIM] = (
                o * lax.rsqrt(ms + LN_EPS) * g_ref[...] * (1.0 - lam_init))


def _attn_sample(q8, k_new, v_new, cache_k, cache_v, page_table, lams, subln_g, lam_init, pages_per_step):
    nbd, n_pages = page_table.shape
    width = q8.shape[-1]

    def page_spec(p):
        return pl.BlockSpec((None, PAGE_SIZE * N_HEADS, V_DIM),
                            lambda b, n, pt: (pt[b, n * pages_per_step + p], 0, 0))

    per_b = lambda rows: pl.BlockSpec((None, rows, width), lambda b, n, pt: (b, 0, 0))
    vec = pl.BlockSpec((1, HEAD_DIM), lambda b, n, pt: (0, 0))
    grid_spec = pltpu.PrefetchScalarGridSpec(
        num_scalar_prefetch=1,
        grid=(nbd, n_pages // pages_per_step),
        in_specs=([per_b(8)] + [page_spec(p) for p in range(pages_per_step)] * 2
                  + [per_b(k_new.shape[1]), per_b(v_new.shape[1]), vec, vec, vec, vec,
                     pl.BlockSpec((1, V_DIM), lambda b, n, pt: (0, 0))]),
        out_specs=per_b(8),
        scratch_shapes=[pltpu.VMEM((ROWS, width), BF16), pltpu.VMEM((ROWS, 1), F32),
                        pltpu.VMEM((ROWS, 1), F32), pltpu.VMEM((ROWS, V_DIM), F32)],
    )
    return pl.pallas_call(
        functools.partial(_attn_sample_kernel, pages_per_step=pages_per_step, lam_init=lam_init),
        grid_spec=grid_spec,
        out_shape=jax.ShapeDtypeStruct((nbd, 8, width), F32),
        compiler_params=_params("parallel", "arbitrary"),
        name="attn_sample",
    )(page_table, q8, *([cache_k] * pages_per_step), *([cache_v] * pages_per_step),
      k_new, v_new, *lams, subln_g)


def _discretize_kernel(ar_ref, ai_ref, ldt_ref, br_ref, bi_ref, lbr_ref, lbi_ref, bbr_ref, bbi_ref):
    dt = jnp.exp(ldt_ref[...])
    ar, ai = ar_ref[...], ai_ref[...]
    mag = jnp.exp(ar * dt)
    lb_re, lb_im = mag * jnp.cos(ai * dt), mag * jnp.sin(ai * dt)
    den = ar * ar + ai * ai
    nr, ni = lb_re - 1.0, lb_im
    cr = (nr * ar + ni * ai) / den
    ci = (ni * ar - nr * ai) / den
    br, bi = br_ref[...], bi_ref[...]
    lbr_ref[...] = lb_re
    lbi_ref[...] = lb_im
    bbr_ref[...] = cr * br - ci * bi
    bbi_ref[...] = cr * bi + ci * br


def _discretize(a_re, a_im, log_dt, b_re, b_im):
    rep = lambda a: jnp.repeat(a, SSM_GROUP, axis=0)
    to_rows = lambda b: jnp.transpose(b, (0, 2, 1)).reshape(SSM_GROUPS * SSM_GROUP, SSM_STATE)
    shape = jax.ShapeDtypeStruct((SSM_GROUPS * SSM_GROUP, SSM_STATE), F32)
    lbr, lbi, bbr, bbi = pl.pallas_call(
        _discretize_kernel, out_shape=[shape] * 4, name="ssm_discretize",
    )(rep(a_re), rep(a_im), rep(log_dt[:, None]), to_rows(b_re), to_rows(b_im))
    unrows = lambda b: b.reshape(SSM_GROUPS, SSM_GROUP, SSM_STATE)
    return lbr[::SSM_GROUP], lbi[::SSM_GROUP], unrows(bbr), unrows(bbi)


def _block_diag_weights(bb_re, bb_im, c_re, c_im):
    eye = jnp.eye(GROUPS_PER_BLOCK, dtype=F32)

    def in_w(bb):
        bb = bb.reshape(N_GROUP_BLOCKS, GROUPS_PER_BLOCK, SSM_GROUP, SSM_STATE)
        w = bb[:, :, :, None, :] * eye[None, :, None, :, None]
        return w.reshape(N_GROUP_BLOCKS, MXU_DIM, STATE_BLOCK)

    def out_w(c):
        c = c.reshape(N_GROUP_BLOCKS, GROUPS_PER_BLOCK, SSM_GROUP, SSM_STATE)
        w = jnp.transpose(c, (0, 1, 3, 2))[:, :, :, None, :] * eye[None, :, None, :, None]
        return w.reshape(N_GROUP_BLOCKS, STATE_BLOCK, MXU_DIM)

    w_b = jnp.concatenate([in_w(bb_re), in_w(bb_im)], axis=2).astype(BF16)
    w_c = jnp.concatenate([out_w(c_re), out_w(-c_im)], axis=1).astype(BF16)
    return w_b, w_c


def _ssm_kernel(u_ref, h0r_ref, h0i_ref, lbr_ref, lbi_ref, wb_ref, wc_ref, d_ref,
                y_ref, hr_ref, hi_ref, hs_sc, sr_sc, si_sc, *, nb, steps, unroll):
    j = pl.program_id(1)

    @pl.when(j == 0)
    def _():
        sr_sc[...] = h0r_ref[...]
        si_sc[...] = h0i_ref[...]

    u = u_ref[...]
    hs_sc[...] = jnp.dot(u.astype(BF16), wb_ref[...], preferred_element_type=F32)
    lbr, lbi = lbr_ref[...], lbi_ref[...]

    def step(t, carry):
        hr, hi = carry
        r0 = pl.multiple_of(t * nb, nb)
        bur = hs_sc[pl.ds(r0, nb), :STATE_BLOCK]
        bui = hs_sc[pl.ds(r0, nb), STATE_BLOCK:]
        nr = lbr * hr - lbi * hi + bur
        ni = lbr * hi + lbi * hr + bui
        hs_sc[pl.ds(r0, nb), :STATE_BLOCK] = nr
        hs_sc[pl.ds(r0, nb), STATE_BLOCK:] = ni
        return nr, ni

    hr, hi = lax.fori_loop(0, steps, step, (sr_sc[...], si_sc[...]), unroll=unroll)
    sr_sc[...] = hr
    si_sc[...] = hi
    y_ref[...] = (jnp.dot(hs_sc[...].astype(BF16), wc_ref[...], preferred_element_type=F32)
                  + d_ref[...] * u)

    @pl.when(j == pl.num_programs(1) - 1)
    def _():
        hr_ref[...] = hr
        hi_ref[...] = hi


def _ssm(u, h0_re, h0_im, lb_re, lb_im, w_b, w_c, d_skip, nb, steps):
    rows = u.shape[0]
    tile = steps * nb
    ublk = pl.BlockSpec((tile, MXU_DIM), lambda g, j: (j, g))
    sblk = pl.BlockSpec((nb, STATE_BLOCK), lambda g, j: (0, g))
    lblk = pl.BlockSpec((1, STATE_BLOCK), lambda g, j: (0, g))
    state = jax.ShapeDtypeStruct(h0_re.shape, F32)
    return pl.pallas_call(
        functools.partial(_ssm_kernel, nb=nb, steps=steps, unroll=(4 if nb <= 8 else True)),
        grid=(N_GROUP_BLOCKS, rows // tile),
        in_specs=[ublk, sblk, sblk, lblk, lblk,
                  pl.BlockSpec((None, MXU_DIM, 2 * STATE_BLOCK), lambda g, j: (g, 0, 0)),
                  pl.BlockSpec((None, 2 * STATE_BLOCK, MXU_DIM), lambda g, j: (g, 0, 0)),
                  pl.BlockSpec((1, MXU_DIM), lambda g, j: (0, g))],
        out_specs=[ublk, sblk, sblk],
        out_shape=[jax.ShapeDtypeStruct(u.shape, F32), state, state],
        scratch_shapes=[pltpu.VMEM((tile, 2 * STATE_BLOCK), F32),
                        pltpu.VMEM((nb, STATE_BLOCK), F32), pltpu.VMEM((nb, STATE_BLOCK), F32)],
        compiler_params=_params("parallel", "arbitrary"),
        name="ssm_scan",
    )(u, h0_re, h0_im, lb_re, lb_im, w_b, w_c, d_skip)


def _merge_kernel(x_ref, o_ref, ga_ref, gs_ref, y_ref, wg_ref, bg_ref, wo_ref, g1_ref, b1_ref,
                  x1_ref, *, alpha):
    gl = _gelu(y_ref[...])
    z = jnp.dot(gl.astype(BF16), wg_ref[...], preferred_element_type=F32) + bg_ref[...]
    s = gl * _sigmoid(z)
    merged = _sigmoid(ga_ref[...]) * o_ref[...] + _sigmoid(gs_ref[...]) * s
    r = alpha * x_ref[...] + jnp.dot(merged.astype(BF16), wo_ref[...], preferred_element_type=F32)
    x1_ref[...] = _layer_norm(r, g1_ref[...], b1_ref[...])


def _merge(x, o, ga, gs, y, w_glu, b_glu, w_out, ln_g, ln_b, alpha, tm):
    nb, seq, d = x.shape
    tmaj = pl.BlockSpec((tm, d), lambda b, i: (i, b))
    bmaj = pl.BlockSpec((None, tm, d), lambda b, i: (b, i, 0))
    return pl.pallas_call(
        functools.partial(_merge_kernel, alpha=alpha),
        grid=(nb, seq // tm),
        in_specs=[bmaj, tmaj, tmaj, tmaj, tmaj, _full((d, d)), _full((1, d)), _full((d, d)),
                  _full((1, d)), _full((1, d))],
        out_specs=tmaj,
        out_shape=jax.ShapeDtypeStruct((seq, nb * d), F32),
        compiler_params=_params("parallel", "parallel"),
        name="merge_ln1",
    )(x, o, ga, gs, y, w_glu, b_glu, w_out, ln_g, ln_b)


def _ffn_kernel(x1_ref, buf_ref, wup_ref, cw_ref, cb_ref, wdn_ref, g2_ref, b2_ref,
                x2_ref, nbuf_ref, carry_sc, *, nb, alpha, d_ff):
    i = pl.program_id(0)
    rows = x1_ref.shape[0]

    @pl.when(i == 0)
    def _():
        carry_sc[...] = buf_ref[...]

    x1 = x1_ref[...]
    xb = x1.astype(BF16)
    f = None
    for c in range(d_ff // FF_CHUNK):
        lo, hi = c * FF_CHUNK, (c + 1) * FF_CHUNK
        a = jnp.dot(xb, wup_ref[:, lo:hi], preferred_element_type=F32)
        gate = jnp.dot(xb, wup_ref[:, d_ff + lo:d_ff + hi], preferred_element_type=F32)
        ap = jnp.concatenate([carry_sc[:, lo:hi], a], axis=0)
        conv = (cb_ref[:, lo:hi] + ap[:rows] * cw_ref[0:1, lo:hi]
                + ap[nb:rows + nb] * cw_ref[1:2, lo:hi] + ap[2 * nb:] * cw_ref[2:3, lo:hi])
        carry_sc[:, lo:hi] = ap[rows:]
        h = _gelu(conv) * gate
        d = jnp.dot(h.astype(BF16), wdn_ref[lo:hi, :], preferred_element_type=F32)
        f = d if f is None else f + d
    x2_ref[...] = _layer_norm(alpha * x1 + f, g2_ref[...], b2_ref[...])

    @pl.when(i == pl.num_programs(0) - 1)
    def _():
        nbuf_ref[...] = carry_sc[...]


def _ffn(x1, buf, w_up, conv_w, conv_b, w_down, ln_g, ln_b, nb, alpha, tile):
    rows, d = x1.shape
    d_ff = w_down.shape[0]
    xblk = pl.BlockSpec((tile, d), lambda i: (i, 0))
    return pl.pallas_call(
        functools.partial(_ffn_kernel, nb=nb, alpha=alpha, d_ff=d_ff),
        grid=(rows // tile,),
        in_specs=[xblk, _full(buf.shape), _full(w_up.shape), _full(conv_w.shape), _full((1, d_ff)),
                  _full(w_down.shape), _full((1, d)), _full((1, d))],
        out_specs=[xblk, _full(buf.shape)],
        out_shape=[jax.ShapeDtypeStruct(x1.shape, F32), jax.ShapeDtypeStruct(buf.shape, F32)],
        scratch_shapes=[pltpu.VMEM(buf.shape, F32)],
        compiler_params=_params("arbitrary"),
        name="convffn_ln2",
    )(x1, buf, w_up, conv_w, conv_b, w_down, ln_g, ln_b)


def _layer_forward(x, nb, seq, attend, h_re, h_im, conv_buf, alpha, w, *, row_tile, ssm_steps):
    d = x.shape[-1]
    d_ff = w["w_down"].shape[0]
    q, qt, k_f32, k, v_f32, vt, u, ga, gs = _inproj(x, w["w_in"], row_tile // 2)

    o = attend(q, qt, k, v_f32, vt)

    y, hr, hi = _ssm(u.reshape(seq * nb, d), h_re.reshape(nb, -1), h_im.reshape(nb, -1),
                     w["lb_re"], w["lb_im"], w["ssm_w_b"], w["ssm_w_c"], w["ssm_d"], nb, ssm_steps)

    x1 = _merge(x, o, ga, gs, y.reshape(q.shape), w["w_glu"], w["b_glu"], w["w_out"],
                w["ln1_g"], w["ln1_b"], alpha, row_tile)

    buf = jnp.transpose(conv_buf, (1, 0, 2)).reshape((CONV_W - 1) * nb, d_ff)
    x2, nbuf = _ffn(x1.reshape(seq * nb, d), buf, w["w_up"], w["conv_w"], w["conv_b"], w["w_down"],
                    w["ln2_g"], w["ln2_b"], nb, alpha, row_tile)

    nbuf = jnp.transpose(nbuf.reshape(CONV_W - 1, nb, d_ff), (1, 0, 2))
    shape = (nb, SSM_GROUPS, SSM_STATE)
    return x2, k_f32, v_f32, hr.reshape(shape), hi.reshape(shape), nbuf


def kernel(x_prompt, x_sample, cache_k, cache_v, state_ssm_re, state_ssm_im, state_conv, page_table, w_in, lambda_q1, lambda_k1, lambda_q2, lambda_k2, subln_g, ssm_a_re, ssm_a_im, ssm_log_dt, ssm_b_re, ssm_b_im, ssm_c_re, ssm_c_im, ssm_d, w_glu, b_glu, w_out, ln1_g, ln1_b, w_up, conv_w, conv_b, w_down, ln2_g, ln2_b):
    depth = w_in.shape[0]
    alpha = (2.0 * depth) ** 0.25
    bp, seq, d = x_prompt.shape
    bd, dec_seq, _ = x_sample.shape
    d_ff = w_down.shape[1]
    row_tile = 512
    assert dec_seq == 4 and seq % row_tile == 0 and (bd * dec_seq) % row_tile == 0

    yp = x_prompt
    ys = jnp.transpose(x_sample, (1, 0, 2)).reshape(1, dec_seq * bd, d)
    outs = {name: [] for name in ("kp", "vp", "ks", "vs", "hrp", "hip", "hrs", "his", "cp", "cs")}
    for layer in range(depth):
        lam_init = 0.8 - 0.6 * math.exp(-0.3 * layer)
        row = lambda a: a[layer].reshape(1, -1)
        lb_re, lb_im, bb_re, bb_im = _discretize(ssm_a_re[layer], ssm_a_im[layer], ssm_log_dt[layer],
                                                  ssm_b_re[layer], ssm_b_im[layer])
        ssm_w_b, ssm_w_c = _block_diag_weights(bb_re, bb_im, ssm_c_re[layer], ssm_c_im[layer])
        w = dict(
            w_in=w_in[layer].astype(BF16), lb_re=lb_re.reshape(1, -1), lb_im=lb_im.reshape(1, -1),
            ssm_w_b=ssm_w_b, ssm_w_c=ssm_w_c, ssm_d=row(ssm_d),
            w_glu=w_glu[layer].astype(BF16), b_glu=row(b_glu), w_out=w_out[layer].astype(BF16),
            ln1_g=row(ln1_g), ln1_b=row(ln1_b), w_up=w_up[layer].astype(BF16),
            conv_w=conv_w[layer], conv_b=row(conv_b), w_down=w_down[layer].astype(BF16),
            ln2_g=row(ln2_g), ln2_b=row(ln2_b))
        lams = (row(lambda_q1), row(lambda_k1), row(lambda_q2), row(lambda_k2))
        g_sub = row(subln_g)

        attend_p = lambda q, qt, k, v, vt: _attn_prompt(qt, k, vt, lams, g_sub, bp, lam_init, 4)
        zeros_h = jnp.zeros((bp, SSM_GROUPS, SSM_STATE), F32)
        yp, kp, vp, hrp, hip, cbp = _layer_forward(
            yp, bp, seq, attend_p, zeros_h, zeros_h, jnp.zeros((bp, CONV_W - 1, d_ff), F32), alpha, w,
            row_tile=row_tile, ssm_steps=128)
        yp = jnp.transpose(yp.reshape(seq, bp, d), (1, 0, 2))

        n_pool = cache_k.shape[1]
        as_pages = lambda c: c.reshape(depth * n_pool, PAGE_SIZE * N_HEADS, V_DIM)
        pages = page_table + layer * n_pool
        per_seq = lambda a: jnp.transpose(a.reshape(dec_seq, bd, d), (1, 0, 2))

        def attend_s(q, qt, k, v, vt):
            qs = per_seq(q).astype(F32)
            q8 = jnp.concatenate([qs, qs], axis=1)
            pad = lambda a: jnp.pad(per_seq(a).astype(BF16), ((0, 0), (0, 16 - dec_seq), (0, 0)))
            o8 = _attn_sample(q8, pad(k), pad(v), as_pages(cache_k), as_pages(cache_v), pages,
                              lams, g_sub, lam_init, 16)
            return jnp.transpose(o8[:, :dec_seq], (1, 0, 2)).reshape(dec_seq * bd, d)

        ys, ks, vs, hrs, his, cbs = _layer_forward(
            ys, bd, dec_seq, attend_s, state_ssm_re[layer], state_ssm_im[layer], state_conv[layer],
            alpha, w, row_tile=row_tile, ssm_steps=dec_seq)
        ys = ys.reshape(1, dec_seq * bd, d)
        ks, vs = per_seq(ks), per_seq(vs)
        for name, val in zip(("kp", "vp", "ks", "vs", "hrp", "hip", "hrs", "his", "cp", "cs"),
                             (kp, vp, ks, vs, hrp, hip, hrs, his, cbp, cbs)):
            outs[name].append(val)

    y_sample = per_seq(ys)
    st = lambda name: jnp.stack(outs[name])
    heads = lambda a: a.reshape(a.shape[:-1] + (N_HEADS, V_DIM))
    return (yp, y_sample, heads(st("kp")), heads(st("vp")), heads(st("ks")), heads(st("vs")),
            st("hrp"), st("hip"), st("hrs"), st("his"), st("cp"), st("cs"))
```
